```python
import jax
import jax.numpy as jnp
from jax import lax
import numpy as np

D_MODEL = 1024
BATCH = 4
SEQ = 4096
DEPTH = 1
DEC_BATCH = 128
DEC_SEQ = 1
PAST_LEN = 8192
PAGE_SIZE = 128

HEAD_DIM = 64
FOX_HEADS = 8
DSA_HEADS = 8
FOX_W = FOX_HEADS * HEAD_DIM
DSA_W = DSA_HEADS * HEAD_DIM
IDX_HEADS = 8
IDX_DIM = 64
IDX_SCORE_SCALE = (IDX_DIM ** -0.5) * (IDX_HEADS ** -0.5)
TOPK_MAX = 256
MEM_TOKENS = 256
MEM_HEADS = 4
MEM_W = MEM_HEADS * HEAD_DIM
D_FF = -(-8 * D_MODEL // (3 * 256)) * 256
ROPE_THETA = 500000.0
ROT_FRACTION = 4
Q_BLOCK = 128
LN_EPS = 1e-5
DEEPNORM_ALPHA = (2 * DEPTH) ** 0.25
DEEPNORM_BETA = (8 * DEPTH) ** -0.25

IN_WIDTHS = [FOX_W, FOX_W, FOX_W, FOX_HEADS,
             DSA_W, DSA_W, DSA_W,
             IDX_HEADS * IDX_DIM, IDX_DIM, IDX_HEADS,
             D_MODEL, D_MODEL]
IN_COLS = int(sum(IN_WIDTHS))
IN_OFFSETS = [int(o) for o in np.cumsum(IN_WIDTHS)[:-1]]

kernel_name = 'hybrid_fox_dsa_memxattn_decoder_step'


def layer_norm(x, g, b):
    xf = x.astype(jnp.float32)
    mu = xf.mean(-1, keepdims=True)
    var = jnp.square(xf - mu).mean(-1, keepdims=True)
    return ((xf - mu) * lax.rsqrt(var + LN_EPS) * g.astype(jnp.float32) + b.astype(jnp.float32)).astype(x.dtype)


def rope_partial(x, pos):
    rot = x.shape[-1] // ROT_FRACTION
    half = rot // 2
    inv_freq = ROPE_THETA ** (-jnp.arange(half, dtype=jnp.float32) / half)
    ang = pos.astype(jnp.float32)[:, None] * inv_freq[None, :]
    cos = jnp.cos(ang)[:, None, :]
    sin = jnp.sin(ang)[:, None, :]
    x1 = x[..., :half].astype(jnp.float32)
    x2 = x[..., half:rot].astype(jnp.float32)
    return jnp.concatenate([(x1 * cos - x2 * sin).astype(x.dtype), (x2 * cos + x1 * sin).astype(x.dtype), x[..., rot:]], axis=-1)


def take_rows(rows, idx):
    return jax.vmap(lambda r, i: r[i])(rows, idx)


def in_proj(x, w_in, b_fgate, pos):
    B, S, _ = x.shape
    z = x @ w_in
    fq, fk, fv, fg, dq, dk, dv, iq, ik, iw, ga, gb = jnp.split(z, IN_OFFSETS, axis=-1)
    heads = lambda t, h: t.reshape(B, S, h, -1)
    logf = jax.nn.log_sigmoid(fg.astype(jnp.float32) + b_fgate.astype(jnp.float32))
    dq = rope_partial(heads(dq, DSA_HEADS), pos)
    dk = rope_partial(heads(dk, DSA_HEADS), pos)
    iq = rope_partial(heads(iq, IDX_HEADS), pos)
    ik = rope_partial(ik[:, :, None, :], pos)[:, :, 0, :]
    return (heads(fq, FOX_HEADS), heads(fk, FOX_HEADS), heads(fv, FOX_HEADS), logf,
            dq, dk, heads(dv, DSA_HEADS), iq, ik, iw, ga, gb)


def fox_bias(cq, ck):
    return cq.transpose(0, 2, 1)[..., :, None] - ck.transpose(0, 2, 1)[..., None, :]


def fox_prompt(q, k, v, logf):
    B, S, H, Dh = q.shape
    nb = S // Q_BLOCK
    scale = Dh ** -0.5
    c = jnp.cumsum(logf, axis=1)
    kpos = jnp.arange(S)
    blk = lambda t: t.reshape(B, nb, Q_BLOCK, *t.shape[2:]).swapaxes(0, 1)

    def block(args):
        qi, ci, ti = args
        s = jnp.einsum('bthd,bshd->bhts', qi, k, preferred_element_type=jnp.float32) * scale + fox_bias(ci, c)
        s = jnp.where(kpos[None, :] <= ti[:, None], s, -jnp.inf)
        p = jax.nn.softmax(s, axis=-1)
        return jnp.einsum('bhts,bshd->bthd', p, v).astype(q.dtype)

    o = lax.map(block, (blk(q), blk(c), jnp.arange(S).reshape(nb, Q_BLOCK)))
    return o.swapaxes(0, 1).reshape(B, S, H, Dh)


def fox_decode(q, k_new, v_new, logf_new, cache_k, cache_v, cache_logf, page_table):
    N, T, H, Dh = q.shape
    n_pages = page_table.shape[1]
    page = cache_k.shape[1]
    past = n_pages * page
    scale = Dh ** -0.5
    logf_past = cache_logf[page_table].reshape(N, past, H).astype(jnp.float32)
    c = jnp.cumsum(jnp.concatenate([logf_past, logf_new], axis=1), axis=1)
    cq = c[:, past:]
    s = jnp.einsum('nthd,nshd->nhts', q, k_new, preferred_element_type=jnp.float32) * scale + fox_bias(cq, cq)
    s = jnp.where(jnp.tril(jnp.ones((T, T), bool)), s, -jnp.inf)
    m_new = s.max(-1)
    p = jnp.exp(s - m_new[..., None])
    l_new = p.sum(-1)
    a_new = jnp.einsum('nhts,nshd->nhtd', p, v_new)
    c_pages = c[:, :past].reshape(N, n_pages, page, H).swapaxes(0, 1)

    def page_part(args):
        phys, ck = args
        sp = jnp.einsum('nthd,nshd->nhts', q, cache_k[phys], preferred_element_type=jnp.float32) * scale + fox_bias(cq, ck)
        mp = sp.max(-1)
        pp = jnp.exp(sp - mp[..., None])
        return mp, pp.sum(-1), jnp.einsum('nhts,nshd->nhtd', pp, cache_v[phys])

    m_p, l_p, a_p = lax.map(page_part, (page_table.T, c_pages))
    m_all = jnp.concatenate([m_p, m_new[None]], axis=0)
    l_all = jnp.concatenate([l_p, l_new[None]], axis=0)
    a_all = jnp.concatenate([a_p.astype(jnp.float32), a_new[None].astype(jnp.float32)], axis=0)
    m = m_all.max(0)
    w = jnp.exp(m_all - m[None])
    o = (w[..., None] * a_all).sum(0) / (w * l_all).sum(0)[..., None]
    return o.transpose(0, 2, 1, 3).astype(q.dtype)


def index_scores(iq, iw, ik):
    r = jax.nn.relu(jnp.einsum('bthd,bsd->bths', iq, ik, preferred_element_type=jnp.float32))
    return jnp.einsum('bths,bth->bts', r, iw.astype(jnp.float32)) * IDX_SCORE_SCALE


def sparse_attend(q, k_sel, v_sel, valid):
    s = jnp.einsum('bthd,btkhd->bhtk', q, k_sel, preferred_element_type=jnp.float32) * q.shape[-1] ** -0.5
    s = jnp.where(valid[:, None], s, -jnp.inf)
    p = jax.nn.softmax(s, axis=-1)
    return jnp.einsum('bhtk,btkhd->bthd', p, v_sel).astype(q.dtype)


def dsa_prompt(q, k, v, iq, ik, iw, k_sel):
    B, S, H, Dh = q.shape
    nb = S // Q_BLOCK
    kpos = jnp.arange(S)
    blk = lambda t: t.reshape(B, nb, Q_BLOCK, *t.shape[2:]).swapaxes(0, 1)

    def block(args):
        qi, iqi, iwi, ti = args
        sc = index_scores(iqi, iwi, ik)
        sc = jnp.where(kpos[None, None, :] <= ti[None, :, None], sc, -jnp.inf)
        _, idx = lax.top_k(sc, k_sel)
        valid = idx <= ti[None, :, None]
        return sparse_attend(qi, take_rows(k, idx), take_rows(v, idx), valid)

    o = lax.map(block, (blk(q), blk(iq), blk(iw), jnp.arange(S).reshape(nb, Q_BLOCK)))
    return o.swapaxes(0, 1).reshape(B, S, H, Dh)


def dsa_decode(q, k_new, v_new, iq, ik_new, iw, cache_k, cache_v, cache_ik, page_table, k_sel):
    N, T = q.shape[:2]
    n_pages = page_table.shape[1]
    page = cache_k.shape[1]
    past = n_pages * page
    ik_all = jnp.concatenate([cache_ik[page_table].reshape(N, past, -1).astype(ik_new.dtype), ik_new], axis=1)
    qpos = past + jnp.arange(T)
    kpos = jnp.arange(past + T)
    sc = index_scores(iq, iw, ik_all)
    sc = jnp.where(kpos[None, None, :] <= qpos[None, :, None], sc, -jnp.inf)
    _, idx = lax.top_k(sc, k_sel)
    valid = idx <= qpos[None, :, None]
    in_past = idx < past
    pidx = jnp.minimum(idx, past - 1)
    phys = jnp.take_along_axis(page_table, (pidx // page).reshape(N, -1), axis=1).reshape(idx.shape)
    off = pidx % page
    nidx = jnp.clip(idx - past, 0, T - 1)

    def select(cache, new):
        return jnp.where(in_past[..., None, None], cache[phys, off].astype(new.dtype), take_rows(new, nidx))

    return sparse_attend(q, select(cache_k, k_new), select(cache_v, v_new), valid)


def branch_merge(o_fox, o_dsa, ga, gb, w_fox_up, w_dsa_up, w_mix_out):
    B, S = o_fox.shape[:2]
    u = (jax.nn.sigmoid(ga) * (o_fox.reshape(B, S, FOX_W) @ w_fox_up)
         + jax.nn.sigmoid(gb) * (o_dsa.reshape(B, S, DSA_W) @ w_dsa_up))
    return u @ w_mix_out


def mem_kv(mem, w_mkv):
    B, M, _ = mem.shape
    mk, mv = jnp.split(mem @ w_mkv, 2, axis=-1)
    return mk.reshape(B, M, MEM_HEADS, HEAD_DIM), mv.reshape(B, M, MEM_HEADS, HEAD_DIM)


def mem_cross_attn(h, mk, mv, w_mq, w_mo):
    B, S, _ = h.shape
    q = (h @ w_mq).reshape(B, S, MEM_HEADS, HEAD_DIM)
    s = jnp.einsum('bshd,bmhd->bhsm', q, mk, preferred_element_type=jnp.float32) * HEAD_DIM ** -0.5
    p = jax.nn.softmax(s, axis=-1)
    o = jnp.einsum('bhsm,bmhd->bshd', p, mv).astype(h.dtype)
    return o.reshape(B, S, MEM_W) @ w_mo


def swiglu(h, w_gate, w_up, w_down):
    return (jax.nn.silu(h @ w_gate) * (h @ w_up)) @ w_down


def post_sublayers(x, a, mk, mv, ln1_g, ln1_b, w_mq, w_mo, ln2_g, ln2_b, w_ffn_gate, w_ffn_up, w_ffn_down, ln3_g, ln3_b):
    h = layer_norm(DEEPNORM_ALPHA * x + a, ln1_g, ln1_b)
    h = layer_norm(DEEPNORM_ALPHA * h + mem_cross_attn(h, mk, mv, w_mq, w_mo), ln2_g, ln2_b)
    return layer_norm(DEEPNORM_ALPHA * h + swiglu(h, w_ffn_gate, w_ffn_up, w_ffn_down), ln3_g, ln3_b)


def setup_inputs(seed: int = 0) -> dict:
    key = jax.random.key(seed)
    ks = iter(jax.random.split(key, 40))
    nrm = lambda shape, scale=1.0: jax.random.normal(next(ks), shape, jnp.float32) * scale
    n_pages = PAST_LEN // PAGE_SIZE
    n_used = DEC_BATCH * n_pages
    n_pool = n_used + max(1, n_used // 4)
    page_table = jax.random.permutation(next(ks), n_pool)[:n_used].reshape(DEC_BATCH, n_pages).astype(jnp.int32)
    fgate_base = jnp.linspace(1.0, 6.0, FOX_HEADS, dtype=jnp.float32)
    L = DEPTH
    return {
        'x_prompt': nrm((BATCH, SEQ, D_MODEL)),
        'x_sample': nrm((DEC_BATCH, DEC_SEQ, D_MODEL)),
        'cache_fox_k': nrm((L, n_pool, PAGE_SIZE, FOX_HEADS, HEAD_DIM)),
        'cache_fox_v': nrm((L, n_pool, PAGE_SIZE, FOX_HEADS, HEAD_DIM)),
        'cache_fox_logf': jax.nn.log_sigmoid(fgate_base + nrm((L, n_pool, PAGE_SIZE, FOX_HEADS), 0.5)),
        'cache_dsa_k': nrm((L, n_pool, PAGE_SIZE, DSA_HEADS, HEAD_DIM)),
        'cache_dsa_v': nrm((L, n_pool, PAGE_SIZE, DSA_HEADS, HEAD_DIM)),
        'cache_idx_k': nrm((L, n_pool, PAGE_SIZE, IDX_DIM)),
        'cache_mem_k': nrm((L, DEC_BATCH, MEM_TOKENS, MEM_HEADS, HEAD_DIM)),
        'cache_mem_v': nrm((L, DEC_BATCH, MEM_TOKENS, MEM_HEADS, HEAD_DIM)),
        'page_table': page_table,
        'mem_prompt': nrm((BATCH, MEM_TOKENS, D_MODEL)),
        'w_in': nrm((L, D_MODEL, IN_COLS), D_MODEL ** -0.5),
        'b_fgate': fgate_base + nrm((L, FOX_HEADS), 0.1),
        'w_fox_up': nrm((L, FOX_W, D_MODEL), FOX_W ** -0.5),
        'w_dsa_up': nrm((L, DSA_W, D_MODEL), DSA_W ** -0.5),
        'w_mix_out': nrm((L, D_MODEL, D_MODEL), D_MODEL ** -0.5 * DEEPNORM_BETA),
        'ln1_g': 1.0 + nrm((L, D_MODEL), 0.02),
        'ln1_b': nrm((L, D_MODEL), 0.02),
        'w_mq': nrm((L, D_MODEL, MEM_W), D_MODEL ** -0.5),
        'w_mkv': nrm((L, D_MODEL, 2 * MEM_W), D_MODEL ** -0.5),
        'w_mo': nrm((L, MEM_W, D_MODEL), MEM_W ** -0.5 * DEEPNORM_BETA),
        'ln2_g': 1.0 + nrm((L, D_MODEL), 0.02),
        'ln2_b': nrm((L, D_MODEL), 0.02),
        'w_ffn_gate': nrm((L, D_MODEL, D_FF), D_MODEL ** -0.5),
        'w_ffn_up': nrm((L, D_MODEL, D_FF), D_MODEL ** -0.5),
        'w_ffn_down': nrm((L, D_FF, D_MODEL), D_FF ** -0.5 * DEEPNORM_BETA),
        'ln3_g': 1.0 + nrm((L, D_MODEL), 0.02),
        'ln3_b': nrm((L, D_MODEL), 0.02),
    }


def reference(x_prompt, x_sample, cache_fox_k, cache_fox_v, cache_fox_logf, cache_dsa_k, cache_dsa_v, cache_idx_k,
              cache_mem_k, cache_mem_v, page_table, mem_prompt, w_in, b_fgate, w_fox_up, w_dsa_up, w_mix_out,
              ln1_g, ln1_b, w_mq, w_mkv, w_mo, ln2_g, ln2_b, w_ffn_gate, w_ffn_up, w_ffn_down, ln3_g, ln3_b):
    S = x_prompt.shape[1]
    T = x_sample.shape[1]
    past = page_table.shape[1] * cache_fox_k.shape[2]
    pos_p = jnp.arange(S)
    pos_s = past + jnp.arange(T)
    k_sel_p = min(TOPK_MAX, S // 4)
    k_sel_s = min(TOPK_MAX, (past + T) // 4)
    hp, hs = x_prompt, x_sample
    fkp, fvp, flp, dkp, dvp, ikp, mkp, mvp = [], [], [], [], [], [], [], []
    fks, fvs, fls, dks, dvs, iks = [], [], [], [], [], []
    for l in range(DEPTH):
        rest = (ln1_g[l], ln1_b[l], w_mq[l], w_mo[l], ln2_g[l], ln2_b[l],
                w_ffn_gate[l], w_ffn_up[l], w_ffn_down[l], ln3_g[l], ln3_b[l])
        fq, fk, fv, logf, dq, dk, dv, iq, ik, iw, ga, gb = in_proj(hp, w_in[l], b_fgate[l], pos_p)
        o_f = fox_prompt(fq, fk, fv, logf)
        o_d = dsa_prompt(dq, dk, dv, iq, ik, iw, k_sel_p)
        a = branch_merge(o_f, o_d, ga, gb, w_fox_up[l], w_dsa_up[l], w_mix_out[l])
        mk, mv = mem_kv(mem_prompt, w_mkv[l])
        hp = post_sublayers(hp, a, mk, mv, *rest)
        fkp.append(fk); fvp.append(fv); flp.append(logf); dkp.append(dk); dvp.append(dv); ikp.append(ik)
        mkp.append(mk); mvp.append(mv)
        fq, fk, fv, logf, dq, dk, dv, iq, ik, iw, ga, gb = in_proj(hs, w_in[l], b_fgate[l], pos_s)
        o_f = fox_decode(fq, fk, fv, logf, cache_fox_k[l], cache_fox_v[l], cache_fox_logf[l], page_table)
        o_d = dsa_decode(dq, dk, dv, iq, ik, iw, cache_dsa_k[l], cache_dsa_v[l], cache_idx_k[l], page_table, k_sel_s)
        a = branch_merge(o_f, o_d, ga, gb, w_fox_up[l], w_dsa_up[l], w_mix_out[l])
        hs = post_sublayers(hs, a, cache_mem_k[l], cache_mem_v[l], *rest)
        fks.append(fk); fvs.append(fv); fls.append(logf); dks.append(dk); dvs.append(dv); iks.append(ik)
    return (hp, hs,
            jnp.stack(fkp), jnp.stack(fvp), jnp.stack(flp), jnp.stack(dkp), jnp.stack(dvp), jnp.stack(ikp),
            jnp.stack(mkp), jnp.stack(mvp),
            jnp.stack(fks), jnp.stack(fvs), jnp.stack(fls), jnp.stack(dks), jnp.stack(dvs), jnp.stack(iks))
```

```python
import functools

import numpy as np
import jax
import jax.numpy as jnp
from jax import lax
from jax.experimental import pallas as pl
from jax.experimental.pallas import tpu as pltpu

F32 = jnp.float32
BF16 = jnp.bfloat16
I32 = jnp.int32

LANE = 128
VMEM_LIMIT_BYTES = 56 * 1024 * 1024

HEAD_DIM = 64
N_HEADS = 8
MEM_HEADS = 4
ROT_HALF = 8
ROPE_THETA = 500000.0
LN_EPS = 1e-5
TOPK_MAX = 256
DEEPNORM_ALPHA = 2.0 ** 0.25
QK_SCALE = HEAD_DIM ** -0.5
IDX_SCORE_SCALE = (HEAD_DIM ** -0.5) * (N_HEADS ** -0.5)
FF_CHUNK = 256

NEG = -1e30
INT_MIN = -(2 ** 31)

HW = N_HEADS * HEAD_DIM
IN_WIDTHS = [HW, HW, HW, N_HEADS, HW, HW, HW, HW, HEAD_DIM, N_HEADS, 1024, 1024]
IN_OFFSETS = [int(o) for o in np.cumsum(IN_WIDTHS)[:-1]]

C_FQ, C_FK, C_FV, C_DQ, C_DK, C_DV, C_IQ, C_GA, C_GB, C_IK, C_MISC, C_END = (
    0, 512, 1024, 1536, 2048, 2560, 3072, 3584, 4608, 5632, 5760, 5888)
MISC_IW = 0
MISC_LOGF = 8


def _cparams(*sem):
    return pltpu.CompilerParams(dimension_semantics=sem or None, vmem_limit_bytes=VMEM_LIMIT_BYTES)


def _dot(a, b):
    return jnp.dot(a, b, preferred_element_type=F32)


def _dot_nt(a, b):
    return lax.dot_general(a, b, (((1,), (1,)), ((), ())), preferred_element_type=F32)


def _sigmoid(x):
    return 1.0 / (1.0 + jnp.exp(-x))


def _layer_norm(x, g, b):
    mu = jnp.mean(x, axis=-1, keepdims=True)
    xc = x - mu
    var = jnp.mean(xc * xc, axis=-1, keepdims=True)
    return xc * lax.rsqrt(var + LN_EPS) * g + b


def _sortable(x):
    bits = lax.bitcast_convert_type(x + 0.0, I32)
    return jnp.where(bits < 0, bits ^ 0x7FFFFFFF, bits)


def _half_mask(q2, odd):
    lane = lax.broadcasted_iota(I32, q2.shape, 1)
    keep = (lane >= HEAD_DIM) if odd else (lane < HEAD_DIM)
    return jnp.where(keep, q2.astype(F32), 0.0).astype(BF16)


def _inproj_kernel(x_ref, w_ref, cos_ref, sup_ref, sdn_ref, b_ref,
                   fq_ref, fk32_ref, fk16_ref, fv32_ref, fv16_ref,
                   dq_ref, dk32_ref, dk16_ref, dv32_ref, dv16_ref,
                   iq_ref, ga_ref, gb_ref, ik32_ref, ik16_ref, misc_ref):
    xb = x_ref[...].astype(BF16)
    cosv, sup, sdn = cos_ref[...], sup_ref[...], sdn_ref[...]

    def proj(off, width):
        return _dot(xb, w_ref[:, off:off + width])

    def rope(z):
        return (z * cosv + pltpu.roll(z, LANE - ROT_HALF, 1) * sup
                + pltpu.roll(z, ROT_HALF, 1) * sdn)

    fq_ref[...] = proj(C_FQ, HW).astype(BF16)
    z = proj(C_FK, HW)
    fk32_ref[...] = z
    fk16_ref[...] = z.astype(BF16)
    z = proj(C_FV, HW)
    fv32_ref[...] = z
    fv16_ref[...] = z.astype(BF16)
    z = proj(C_DQ, HW)
    for g in range(HW // LANE):
        dq_ref[:, g * LANE:(g + 1) * LANE] = rope(z[:, g * LANE:(g + 1) * LANE]).astype(BF16)
    z = proj(C_DK, HW)
    for g in range(HW // LANE):
        r = rope(z[:, g * LANE:(g + 1) * LANE])
        dk32_ref[:, g * LANE:(g + 1) * LANE] = r
        dk16_ref[:, g * LANE:(g + 1) * LANE] = r.astype(BF16)
    z = proj(C_DV, HW)
    dv32_ref[...] = z
    dv16_ref[...] = z.astype(BF16)
    z = proj(C_IQ, HW)
    for g in range(HW // LANE):
        iq_ref[:, g * LANE:(g + 1) * LANE] = rope(z[:, g * LANE:(g + 1) * LANE]).astype(BF16)
    ga_ref[...] = proj(C_GA, 1024)
    gb_ref[...] = proj(C_GB, 1024)
    r = rope(proj(C_IK, LANE))
    ik32_ref[...] = r
    ik16_ref[...] = r.astype(BF16)
    z = proj(C_MISC, LANE)
    lane = lax.broadcasted_iota(I32, z.shape, 1)
    t = z + b_ref[...]
    logf = -(jnp.maximum(-t, 0.0) + jnp.log1p(jnp.exp(-jnp.abs(t))))
    misc_ref[...] = jnp.where(lane < MISC_LOGF, z * IDX_SCORE_SCALE,
                              jnp.where(lane < MISC_LOGF + N_HEADS, logf, 0.0))


def _in_proj(x2, wcat, bpad, tables, tm, table_blocks):
    n, d = x2.shape
    row = lambda i: (i, 0)
    tab = lambda i: (i % table_blocks, 0)
    const = lambda i: (0, 0)
    outs = [(HW, BF16), (HW, F32), (HW, BF16), (HW, F32), (HW, BF16),
            (HW, BF16), (HW, F32), (HW, BF16), (HW, F32), (HW, BF16),
            (HW, BF16), (1024, F32), (1024, F32), (LANE, F32), (LANE, BF16), (LANE, F32)]
    return pl.pallas_call(
        _inproj_kernel,
        grid=(n // tm,),
        in_specs=[pl.BlockSpec((tm, d), row),
                  pl.BlockSpec((d, C_END), const, pipeline_mode=pl.Buffered(1)),
                  pl.BlockSpec((tm, LANE), tab), pl.BlockSpec((tm, LANE), tab),
                  pl.BlockSpec((tm, LANE), tab), pl.BlockSpec((1, LANE), const)],
        out_specs=[pl.BlockSpec((tm, w), row) for w, _ in outs],
        out_shape=[jax.ShapeDtypeStruct((n, w), dt) for w, dt in outs],
        compiler_params=_cparams("arbitrary"),
        name="in_proj",
    )(x2, wcat, *tables, bpad)


def _rope_tables(pos):
    p = pos.shape[0]
    inv_freq = ROPE_THETA ** (-jnp.arange(ROT_HALF, dtype=F32) / ROT_HALF)
    ang = pos.astype(F32)[:, None] * inv_freq[None, :]
    cos, sin = jnp.cos(ang), jnp.sin(ang)
    zeros = lambda w: jnp.zeros((p, w), F32)
    cos64 = jnp.concatenate([cos, cos, jnp.ones((p, HEAD_DIM - 2 * ROT_HALF), F32)], axis=1)
    sup64 = jnp.concatenate([-sin, zeros(HEAD_DIM - ROT_HALF)], axis=1)
    sdn64 = jnp.concatenate([zeros(ROT_HALF), sin, zeros(HEAD_DIM - 2 * ROT_HALF)], axis=1)
    two = lambda t: jnp.concatenate([t, t], axis=1)
    return two(cos64), two(sup64), two(sdn64)


def _scan_kernel(m_ref, crow_ref, ct_ref):
    x = m_ref[...]
    s = x.shape[0]
    row = lax.broadcasted_iota(I32, x.shape, 0)
    k = 1
    while k < s:
        x = x + jnp.where(row >= k, pltpu.roll(x, k, 0), 0.0)
        k *= 2
    crow_ref[...] = x
    xt = x.T
    nblk, _, tk = ct_ref.shape
    for j in range(nblk):
        ct_ref[j] = xt[MISC_LOGF:MISC_LOGF + N_HEADS, j * tk:(j + 1) * tk]


def _fox_scan(misc3, tk):
    b, s, _ = misc3.shape
    return pl.pallas_call(
        _scan_kernel,
        grid=(b,),
        in_specs=[pl.BlockSpec((None, s, LANE), lambda i: (i, 0, 0))],
        out_specs=[pl.BlockSpec((None, s, LANE), lambda i: (i, 0, 0)),
                   pl.BlockSpec((None, s // tk, N_HEADS, tk), lambda i: (i, 0, 0, 0))],
        out_shape=[jax.ShapeDtypeStruct((b, s, LANE), F32),
                   jax.ShapeDtypeStruct((b, s // tk, N_HEADS, tk), F32)],
        compiler_params=_cparams("arbitrary"),
        name="fox_scan",
    )(misc3)


def _flash_update(carry, qh, kb, vb, bias, mask=None):
    m, l, acc = carry
    s = _dot_nt(qh, kb) + bias
    if mask is not None:
        s = jnp.where(mask, s, NEG)
    m_new = jnp.maximum(m, jnp.max(s, axis=1, keepdims=True))
    alpha = jnp.exp(m - m_new)
    p = jnp.exp(s - m_new)
    l = alpha * l + jnp.sum(p, axis=1, keepdims=True)
    acc = alpha * acc + _dot(p.astype(BF16), vb)
    return m_new, l, acc


def _flash_init(tq):
    return (jnp.full((tq, 1), NEG, F32), jnp.zeros((tq, 1), F32), jnp.zeros((tq, LANE), F32))


def _fox_prompt_kernel(q_ref, k_ref, v_ref, crow_ref, ct_ref, o_ref):
    qi = pl.program_id(1)
    tq = q_ref.shape[0]
    tk = ct_ref.shape[2]
    lane = lax.broadcasted_iota(I32, (tq, LANE), 1)
    diag = (lax.broadcasted_iota(I32, (tq, tk), 1) <= lax.broadcasted_iota(I32, (tq, tk), 0))
    for g in range(HW // LANE):
        gs = slice(g * LANE, (g + 1) * LANE)
        q2 = q_ref[:, gs]
        outs = []
        for e in range(2):
            h = 2 * g + e
            qh = _half_mask(q2, e)
            cq = crow_ref[:, MISC_LOGF + h:MISC_LOGF + h + 1]

            def block(j, carry, mask, qh=qh, cq=cq, h=h, gs=gs):
                r0 = pl.multiple_of(j * tk, tk)
                bias = cq - ct_ref[j, h:h + 1, :]
                return _flash_update(carry, qh, k_ref[pl.ds(r0, tk), gs], v_ref[pl.ds(r0, tk), gs],
                                     bias, mask)

            carry = lax.fori_loop(0, qi, lambda j, c: block(j, c, None), _flash_init(tq))
            _, l, acc = block(qi, carry, diag)
            outs.append(acc / l)
        o_ref[:, gs] = jnp.where(lane < HEAD_DIM, outs[0], outs[1]).astype(BF16)


def _fox_prompt(q, k, v, crow, ct, tq):
    b, s, _ = q.shape
    nblk, tk = ct.shape[1], ct.shape[3]
    assert tq == tk
    return pl.pallas_call(
        _fox_prompt_kernel,
        grid=(b, s // tq),
        in_specs=[pl.BlockSpec((None, tq, HW), lambda i, j: (i, j, 0)),
                  pl.BlockSpec((None, s, HW), lambda i, j: (i, 0, 0)),
                  pl.BlockSpec((None, s, HW), lambda i, j: (i, 0, 0)),
                  pl.BlockSpec((None, tq, LANE), lambda i, j: (i, j, 0)),
                  pl.BlockSpec((None, nblk, N_HEADS, tk), lambda i, j: (i, 0, 0, 0))],
        out_specs=pl.BlockSpec((None, tq, HW), lambda i, j: (i, j, 0)),
        out_shape=jax.ShapeDtypeStruct((b, s, HW), BF16),
        compiler_params=_cparams("arbitrary", "arbitrary"),
        name="fox_prompt",
    )(q, k, v, crow, ct)


def _kth_largest(count, ksel, rows):
    c = count(lambda k, col: k >= 0)
    t0 = jnp.where(c >= ksel, jnp.zeros((rows, LANE), I32), jnp.full((rows, LANE), INT_MIN, I32))

    def body(i, t):
        cand = t | lax.shift_left(jnp.int32(1), 30 - i)
        c = count(lambda k, col: k >= cand)
        return jnp.where(c >= ksel, cand, t)

    return lax.fori_loop(0, 31, body, t0)


def _tie_cut(count, t, ksel, nbits, rows):
    need = ksel - count(lambda k, col: k > t)

    def body(i, j):
        cand = j | lax.shift_left(jnp.int32(1), nbits - 1 - i)
        c = count(lambda k, col: (k == t) & (col < cand))
        return jnp.where(c < need, cand, j)

    return lax.fori_loop(0, nbits, body, jnp.zeros((rows, LANE), I32))


def _dsa_prompt_kernel(iq_ref, ik_ref, misc_ref, q_ref, k_ref, v_ref, o_ref,
                       keys_ref, bias_ref, lhs_ref, *, ksel, nbits):
    qi = pl.program_id(1)
    tq = q_ref.shape[0]
    kc = keys_ref.shape[2]
    nch = qi + 1
    lane = lax.broadcasted_iota(I32, (tq, LANE), 1)
    rows_g = qi * tq + lax.broadcasted_iota(I32, (tq, kc), 0)
    cols_l = lax.broadcasted_iota(I32, (tq, kc), 1)

    for h in range(N_HEADS):
        g = h // 2
        lhs_ref[h * tq:(h + 1) * tq, :] = _half_mask(iq_ref[:, g * LANE:(g + 1) * LANE], h % 2)
    wcols = [misc_ref[:, MISC_IW + h:MISC_IW + h + 1] for h in range(N_HEADS)]

    def score_chunk(j, _):
        ikc = ik_ref[pl.ds(pl.multiple_of(j * kc, kc), kc), :]
        sc = jnp.zeros((tq, kc), F32)
        for h in range(N_HEADS):
            sc = sc + jnp.maximum(_dot_nt(lhs_ref[h * tq:(h + 1) * tq, :], ikc), 0.0) * wcols[h]
        keys_ref[j] = jnp.where(j * kc + cols_l <= rows_g, _sortable(sc), INT_MIN)
        return 0

    lax.fori_loop(0, nch, score_chunk, 0)

    def count(pred):
        def body(j, part):
            kj = keys_ref[j]
            for g in range(kc // LANE):
                col = j * kc + g * LANE + lane
                part = part + pred(kj[:, g * LANE:(g + 1) * LANE], col).astype(I32)
            return part
        part = lax.fori_loop(0, nch, body, jnp.zeros((tq, LANE), I32))
        return jnp.sum(part, axis=1, keepdims=True)

    t = _kth_largest(count, ksel, tq)
    jcut = _tie_cut(count, t, ksel, nbits, tq)

    def bias_chunk(j, _):
        kj = keys_ref[j]
        for g in range(kc // LANE):
            gs = slice(g * LANE, (g + 1) * LANE)
            col = j * kc + g * LANE + lane
            kg = kj[:, gs]
            sel = ((kg > t) | ((kg == t) & (col <= jcut))) & (col <= rows_g[:, :LANE])
            bias_ref[j, :, gs] = jnp.where(sel, 0.0, NEG)
        return 0

    lax.fori_loop(0, nch, bias_chunk, 0)

    for g in range(HW // LANE):
        gs = slice(g * LANE, (g + 1) * LANE)
        q2 = q_ref[:, gs]
        outs = []
        for e in range(2):
            qh = _half_mask(q2, e)

            def block(j, carry, qh=qh, gs=gs):
                r0 = pl.multiple_of(j * kc, kc)
                return _flash_update(carry, qh, k_ref[pl.ds(r0, kc), gs], v_ref[pl.ds(r0, kc), gs],
                                     bias_ref[j])

            _, l, acc = lax.fori_loop(0, nch, block, _flash_init(tq))
            outs.append(acc / l)
        o_ref[:, gs] = jnp.where(lane < HEAD_DIM, outs[0], outs[1]).astype(BF16)


def _dsa_prompt(iq, ik2, misc3, q, k, v, tq, ksel):
    b, s, _ = q.shape
    nblk = s // tq
    nbits = max(1, int(s - 1).bit_length())
    qmap = lambda i, j: (i, j, 0)
    full = lambda i, j: (i, 0, 0)
    return pl.pallas_call(
        functools.partial(_dsa_prompt_kernel, ksel=ksel, nbits=nbits),
        grid=(b, nblk),
        in_specs=[pl.BlockSpec((None, tq, HW), qmap),
                  pl.BlockSpec((None, s, LANE), full),
                  pl.BlockSpec((None, tq, LANE), qmap),
                  pl.BlockSpec((None, tq, HW), qmap),
                  pl.BlockSpec((None, s, HW), full),
                  pl.BlockSpec((None, s, HW), full)],
        out_specs=pl.BlockSpec((None, tq, HW), qmap),
        out_shape=jax.ShapeDtypeStruct((b, s, HW), BF16),
        scratch_shapes=[pltpu.VMEM((nblk, tq, tq), I32),
                        pltpu.VMEM((nblk, tq, tq), F32),
                        pltpu.VMEM((N_HEADS * tq, LANE), BF16)],
        compiler_params=_cparams("arbitrary", "arbitrary"),
        name="dsa_prompt",
    )(iq, ik2, misc3, q, k, v)


def _merge_kernel(of_ref, od_ref, ga_ref, gb_ref, x_ref, wf_ref, wd_ref, wm_ref, g_ref, b_ref, wq_ref,
                  h_ref, qm_ref):
    a1 = _dot(of_ref[...].astype(BF16), wf_ref[...])
    a2 = _dot(od_ref[...].astype(BF16), wd_ref[...])
    u = _sigmoid(ga_ref[...]) * a1 + _sigmoid(gb_ref[...]) * a2
    a = _dot(u.astype(BF16), wm_ref[...])
    h = _layer_norm(DEEPNORM_ALPHA * x_ref[...] + a, g_ref[...], b_ref[...])
    h_ref[...] = h
    qm_ref[...] = _dot(h.astype(BF16), wq_ref[...]).astype(BF16)


def _merge(of, od, ga, gb, x2, wf, wd, wm, g1, b1, wq, tm):
    n, d = x2.shape
    mw = wq.shape[1]
    row = lambda i: (i, 0)
    const = lambda i: (0, 0)
    wspec = lambda w: pl.BlockSpec(w.shape, const, pipeline_mode=pl.Buffered(1))
    return pl.pallas_call(
        _merge_kernel,
        grid=(n // tm,),
        in_specs=[pl.BlockSpec((tm, HW), row), pl.BlockSpec((tm, HW), row),
                  pl.BlockSpec((tm, d), row), pl.BlockSpec((tm, d), row), pl.BlockSpec((tm, d), row),
                  wspec(wf), wspec(wd), wspec(wm), wspec(g1), wspec(b1), wspec(wq)],
        out_specs=[pl.BlockSpec((tm, d), row), pl.BlockSpec((tm, mw), row)],
        out_shape=[jax.ShapeDtypeStruct((n, d), F32), jax.ShapeDtypeStruct((n, mw), BF16)],
        compiler_params=_cparams("arbitrary"),
        name="merge_ln1",
    )(of, od, ga, gb, x2, wf, wd, wm, g1, b1, wq)


def _memkv_kernel(x_ref, w_ref, k32_ref, v32_ref, k16_ref, v16_ref):
    z = _dot(x_ref[...].astype(BF16), w_ref[...])
    mw = k32_ref.shape[1]
    k32_ref[...] = z[:, :mw]
    v32_ref[...] = z[:, mw:]
    k16_ref[...] = z[:, :mw].astype(BF16)
    v16_ref[...] = z[:, mw:].astype(BF16)


def _mem_kv(mem2, wkv, tm):
    n, d = mem2.shape
    mw = wkv.shape[1] // 2
    row = lambda i: (i, 0)
    return pl.pallas_call(
        _memkv_kernel,
        grid=(n // tm,),
        in_specs=[pl.BlockSpec((tm, d), row), pl.BlockSpec(wkv.shape, lambda i: (0, 0))],
        out_specs=[pl.BlockSpec((tm, mw), row)] * 4,
        out_shape=[jax.ShapeDtypeStruct((n, mw), F32), jax.ShapeDtypeStruct((n, mw), F32),
                   jax.ShapeDtypeStruct((n, mw), BF16), jax.ShapeDtypeStruct((n, mw), BF16)],
        compiler_params=_cparams("arbitrary"),
        name="mem_kv",
    )(mem2, wkv)


def _mem_prompt_kernel(q_ref, k_ref, v_ref, o_ref):
    tm = q_ref.shape[0]
    lane = lax.broadcasted_iota(I32, (tm, LANE), 1)
    for g in range(q_ref.shape[1] // LANE):
        gs = slice(g * LANE, (g + 1) * LANE)
        q2, kb, vb = q_ref[:, gs], k_ref[:, gs], v_ref[:, gs]
        outs = []
        for e in range(2):
            s = _dot_nt(_half_mask(q2, e), kb)
            p = jnp.exp(s - jnp.max(s, axis=1, keepdims=True))
            outs.append(_dot(p.astype(BF16), vb) / jnp.sum(p, axis=1, keepdims=True))
        o_ref[:, gs] = jnp.where(lane < HEAD_DIM, outs[0], outs[1]).astype(BF16)


def _mem_prompt(qm, mk, mv, tm):
    b, s, mw = qm.shape
    m = mk.shape[1]
    return pl.pallas_call(
        _mem_prompt_kernel,
        grid=(b, s // tm),
        in_specs=[pl.BlockSpec((None, tm, mw), lambda i, j: (i, j, 0)),
                  pl.BlockSpec((None, m, mw), lambda i, j: (i, 0, 0)),
                  pl.BlockSpec((None, m, mw), lambda i, j: (i, 0, 0))],
        out_specs=pl.BlockSpec((None, tm, mw), lambda i, j: (i, j, 0)),
        out_shape=jax.ShapeDtypeStruct((b, s, mw), BF16),
        compiler_params=_cparams("arbitrary", "arbitrary"),
        name="mem_attn_prompt",
    )(qm, mk, mv)


def _block_diag(row_vec, nrows):
    w = row_vec.shape[1]
    rows = lax.broadcasted_iota(I32, (nrows, w), 0)
    lanes = lax.broadcasted_iota(I32, (nrows, w), 1)
    blk = rows == lax.shift_right_logical(lanes, 6)
    return jnp.where(blk, jnp.broadcast_to(row_vec.astype(F32), (nrows, w)), 0.0), blk


def _mem_sample_kernel(q_ref, kt_ref, vt_ref, o_ref):
    qbd, blk = _block_diag(q_ref[...], 8)
    s = _dot(qbd.astype(BF16), kt_ref[...].astype(BF16))
    p = jnp.exp(s - jnp.max(s, axis=1, keepdims=True))
    o = _dot_nt(p.astype(BF16), vt_ref[...].astype(BF16)) / jnp.sum(p, axis=1, keepdims=True)
    o_ref[...] = jnp.sum(jnp.where(blk, o, 0.0), axis=0, keepdims=True)


def _mem_sample(qm3, ck, cv):
    n, mw, m = ck.shape
    return pl.pallas_call(
        _mem_sample_kernel,
        grid=(n,),
        in_specs=[pl.BlockSpec((None, 1, mw), lambda i: (i, 0, 0)),
                  pl.BlockSpec((None, mw, m), lambda i: (i, 0, 0)),
                  pl.BlockSpec((None, mw, m), lambda i: (i, 0, 0))],
        out_specs=pl.BlockSpec((None, 1, mw), lambda i: (i, 0, 0)),
        out_shape=jax.ShapeDtypeStruct((n, 1, mw), F32),
        compiler_params=_cparams("arbitrary"),
        name="mem_attn_sample",
    )(qm3, ck, cv)


def _post_kernel(h_ref, om_ref, wo_ref, g2_ref, b2_ref, wg_ref, wu_ref, wdn_ref, g3_ref, b3_ref, y_ref):
    h2 = _layer_norm(DEEPNORM_ALPHA * h_ref[...] + _dot(om_ref[...].astype(BF16), wo_ref[...]),
                     g2_ref[...], b2_ref[...])
    hb = h2.astype(BF16)
    ff = wg_ref.shape[1]
    acc = jnp.zeros(h2.shape, F32)
    for c in range(ff // FF_CHUNK):
        cs = slice(c * FF_CHUNK, (c + 1) * FF_CHUNK)
        gate = _dot(hb, wg_ref[:, cs])
        act = gate * _sigmoid(gate) * _dot(hb, wu_ref[:, cs])
        acc = acc + _dot(act.astype(BF16), wdn_ref[cs, :])
    y_ref[...] = _layer_norm(DEEPNORM_ALPHA * h2 + acc, g3_ref[...], b3_ref[...])


def _post(h1, om, wo, g2, b2, wg, wu, wdn, g3, b3, tm):
    n, d = h1.shape
    row = lambda i: (i, 0)
    const = lambda i: (0, 0)
    wspec = lambda w: pl.BlockSpec(w.shape, const, pipeline_mode=pl.Buffered(1))
    return pl.pallas_call(
        _post_kernel,
        grid=(n // tm,),
        in_specs=[pl.BlockSpec((tm, d), row), pl.BlockSpec((tm, om.shape[1]), row),
                  wspec(wo), wspec(g2), wspec(b2), wspec(wg), wspec(wu), wspec(wdn), wspec(g3), wspec(b3)],
        out_specs=pl.BlockSpec((tm, d), row),
        out_shape=jax.ShapeDtypeStruct((n, d), F32),
        compiler_params=_cparams("arbitrary"),
        name="post_ffn",
    )(h1, om, wo, g2, b2, wg, wu, wdn, g3, b3)


def _page_update(qbd, kt_ref, vt_ref, bias, m_ref, l_ref, acc_ref):
    s = _dot(qbd, kt_ref[...].astype(BF16)) + bias
    m = m_ref[...]
    m_new = jnp.maximum(m, jnp.max(s, axis=1, keepdims=True))
    alpha = jnp.exp(m - m_new)
    p = jnp.exp(s - m_new)
    l_ref[...] = alpha * l_ref[...] + jnp.sum(p, axis=1, keepdims=True)
    acc_ref[...] = alpha[:, :1] * acc_ref[...] + _dot_nt(p.astype(BF16), vt_ref[...].astype(BF16))
    m_ref[...] = m_new


def _page_finish(o_ref, blk, l_ref, acc_ref):
    o = acc_ref[...] / l_ref[:, :1]
    o_ref[...] = jnp.sum(jnp.where(blk, o, 0.0), axis=0, keepdims=True)


def _fox_decode_kernel(pt_ref, q_ref, kn_ref, vn_ref, lfn_ref, *rest, pg):
    k_refs, v_refs, lf_refs = rest[:pg], rest[pg:2 * pg], rest[2 * pg:3 * pg]
    o_ref, m_ref, l_ref, acc_ref, suf_ref = rest[3 * pg:]
    c = pl.program_id(1)
    qbd_f, blk = _block_diag(q_ref[...], N_HEADS)
    qbd = qbd_f.astype(BF16)

    @pl.when(c == 0)
    def _():
        kn = kn_ref[...].astype(BF16).astype(F32)
        m_ref[...] = jnp.broadcast_to(jnp.sum(qbd_f * kn, axis=1, keepdims=True), m_ref.shape)
        l_ref[...] = jnp.ones(l_ref.shape, F32)
        acc_ref[...] = jnp.broadcast_to(vn_ref[...].astype(BF16).astype(F32), acc_ref.shape)
        suf_ref[...] = jnp.broadcast_to(lfn_ref[...], suf_ref.shape)

    lane = lax.broadcasted_iota(I32, (N_HEADS, LANE), 1)
    for i in range(pg):
        y = lf_refs[i][...]
        k = 1
        while k < LANE:
            y = y + jnp.where(lane + k < LANE, pltpu.roll(y, LANE - k, 1), 0.0)
            k *= 2
        excl = jnp.where(lane < LANE - 1, pltpu.roll(y, LANE - 1, 1), 0.0)
        suf = suf_ref[...]
        _page_update(qbd, k_refs[i], v_refs[i], suf + excl, m_ref, l_ref, acc_ref)
        suf_ref[...] = suf + y[:, :1]

    @pl.when(c == pl.num_programs(1) - 1)
    def _():
        _page_finish(o_ref, blk, l_ref, acc_ref)


def _fox_decode(page_table, q3, kn3, vn3, lfn3, cache_kt, cache_vt, cache_lft, pg):
    n, npages = page_table.shape
    page = cache_kt.shape[2]
    assert page == LANE and npages % pg == 0

    def page_map(i):
        return lambda s, c, pt: (pt[s, npages - 1 - (c * pg + i)], 0, 0)

    seq = lambda s, c, pt: (s, 0, 0)
    in_specs = [pl.BlockSpec((None, 1, HW), seq), pl.BlockSpec((None, 1, HW), seq),
                pl.BlockSpec((None, 1, HW), seq), pl.BlockSpec((None, N_HEADS, 1), seq)]
    in_specs += [pl.BlockSpec((None, HW, page), page_map(i)) for i in range(pg)]
    in_specs += [pl.BlockSpec((None, HW, page), page_map(i)) for i in range(pg)]
    in_specs += [pl.BlockSpec((None, N_HEADS, page), page_map(i)) for i in range(pg)]
    return pl.pallas_call(
        functools.partial(_fox_decode_kernel, pg=pg),
        grid_spec=pltpu.PrefetchScalarGridSpec(
            num_scalar_prefetch=1,
            grid=(n, npages // pg),
            in_specs=in_specs,
            out_specs=pl.BlockSpec((None, 1, HW), seq),
            scratch_shapes=[pltpu.VMEM((N_HEADS, LANE), F32), pltpu.VMEM((N_HEADS, LANE), F32),
                            pltpu.VMEM((N_HEADS, HW), F32), pltpu.VMEM((N_HEADS, LANE), F32)]),
        out_shape=jax.ShapeDtypeStruct((n, 1, HW), F32),
        compiler_params=_cparams("arbitrary", "arbitrary"),
        name="fox_decode",
    )(page_table, q3, kn3, vn3, lfn3, *([cache_kt] * pg), *([cache_vt] * pg), *([cache_lft] * pg))


def _dsa_score_kernel(pt_ref, iq_ref, w_ref, ikn_ref, *rest, pg):
    pages = rest[:pg]
    keys_ref, knew_ref = rest[pg], rest[pg + 1]
    iq = iq_ref[...]
    w = w_ref[...]
    for i in range(pg):
        r = _dot(iq, pages[i][...].astype(BF16))
        keys_ref[i] = _sortable(jnp.sum(jnp.maximum(r, 0.0) * w, axis=0, keepdims=True))

    @pl.when(pl.program_id(1) == 0)
    def _():
        ikn = ikn_ref[...].astype(BF16).astype(F32)
        r = jnp.sum(iq.astype(F32) * ikn, axis=1, keepdims=True)
        sc = jnp.sum(jnp.maximum(r, 0.0) * w, axis=0, keepdims=True)
        lane = lax.broadcasted_iota(I32, knew_ref.shape, 1)
        knew_ref[...] = jnp.where(lane == 0, jnp.broadcast_to(_sortable(sc), knew_ref.shape), INT_MIN)


def _dsa_scores(page_table, iq3, w3, ikn3, cache_ikt, pg):
    n, npages = page_table.shape
    page = cache_ikt.shape[2]
    assert page == LANE and npages % pg == 0

    def page_map(i):
        return lambda s, c, pt: (pt[s, c * pg + i], 0, 0)

    seq = lambda s, c, pt: (s, 0, 0)
    in_specs = [pl.BlockSpec((None, N_HEADS, HEAD_DIM), seq), pl.BlockSpec((None, N_HEADS, 1), seq),
                pl.BlockSpec((None, 1, HEAD_DIM), seq)]
    in_specs += [pl.BlockSpec((None, HEAD_DIM, page), page_map(i)) for i in range(pg)]
    return pl.pallas_call(
        functools.partial(_dsa_score_kernel, pg=pg),
        grid_spec=pltpu.PrefetchScalarGridSpec(
            num_scalar_prefetch=1,
            grid=(n, npages // pg),
            in_specs=in_specs,
            out_specs=[pl.BlockSpec((pg, None, 1, page), lambda s, c, pt: (c, s, 0, 0)),
                       pl.BlockSpec((None, 1, LANE), seq)]),
        out_shape=[jax.ShapeDtypeStruct((npages, n, 1, page), I32),
                   jax.ShapeDtypeStruct((n, 1, LANE), I32)],
        compiler_params=_cparams("arbitrary", "arbitrary"),
        name="dsa_decode_scores",
    )(page_table, iq3, w3, ikn3, *([cache_ikt] * pg))


def _select_kernel(keys_ref, knew_ref, t_ref, j_ref, *, ksel, nbits):
    ng, n, _ = keys_ref.shape
    lane = lax.broadcasted_iota(I32, (n, LANE), 1)

    def count(pred):
        part = lax.fori_loop(0, ng, lambda g, part: part + pred(keys_ref[g], g * LANE + lane).astype(I32),
                             jnp.zeros((n, LANE), I32))
        part = part + pred(knew_ref[...], ng * LANE + lane).astype(I32)
        return jnp.sum(part, axis=1, keepdims=True)

    t = _kth_largest(count, ksel, n)
    t_ref[...] = t
    j_ref[...] = _tie_cut(count, t, ksel, nbits, n)


def _dsa_select(keys3, knew2, ksel):
    ng, n, _ = keys3.shape
    nbits = max(1, int((ng + 1) * LANE - 1).bit_length())
    return pl.pallas_call(
        functools.partial(_select_kernel, ksel=ksel, nbits=nbits),
        out_shape=[jax.ShapeDtypeStruct((n, LANE), I32), jax.ShapeDtypeStruct((n, LANE), I32)],
        compiler_params=_cparams(),
        name="dsa_decode_select",
    )(keys3, knew2)


def _dsa_decode_kernel(pt_ref, q_ref, kn_ref, vn_ref, t_ref, j_ref, keys_ref, knew_ref, *rest, pg, npages):
    k_refs, v_refs = rest[:pg], rest[pg:2 * pg]
    o_ref, m_ref, l_ref, acc_ref = rest[2 * pg:]
    c = pl.program_id(1)
    qbd_f, blk = _block_diag(q_ref[...], N_HEADS)
    qbd = qbd_f.astype(BF16)
    t, jcut = t_ref[...], j_ref[...]
    lane = lax.broadcasted_iota(I32, (1, LANE), 1)

    def selected(key, col):
        return (key > t) | ((key == t) & (col <= jcut))

    @pl.when(c == 0)
    def _():
        sel_new = selected(knew_ref[...], npages * LANE + lane) & (lane == 0)
        on = jnp.max(jnp.where(sel_new, 1.0, 0.0), axis=1, keepdims=True)
        kn = kn_ref[...].astype(BF16).astype(F32)
        s_new = jnp.sum(qbd_f * kn, axis=1, keepdims=True)
        m_ref[...] = jnp.broadcast_to(jnp.where(on > 0.0, s_new, NEG), m_ref.shape)
        l_ref[...] = jnp.broadcast_to(on, l_ref.shape)
        acc_ref[...] = jnp.broadcast_to(on * vn_ref[...].astype(BF16).astype(F32), acc_ref.shape)

    for i in range(pg):
        col = (c * pg + i) * LANE + lane
        bias = jnp.where(selected(keys_ref[i], col), 0.0, NEG)
        _page_update(qbd, k_refs[i], v_refs[i], jnp.broadcast_to(bias, (N_HEADS, LANE)), m_ref, l_ref, acc_ref)

    @pl.when(c == pl.num_programs(1) - 1)
    def _():
        _page_finish(o_ref, blk, l_ref, acc_ref)


def _dsa_decode(page_table, q3, kn3, vn3, t3, j3, keys4, knew3, cache_kt, cache_vt, pg):
    n, npages = page_table.shape
    page = cache_kt.shape[2]
    assert page == LANE and npages % pg == 0

    def page_map(i):
        return lambda s, c, pt: (pt[s, c * pg + i], 0, 0)

    seq = lambda s, c, pt: (s, 0, 0)
    in_specs = [pl.BlockSpec((None, 1, HW), seq), pl.BlockSpec((None, 1, HW), seq),
                pl.BlockSpec((None, 1, HW), seq), pl.BlockSpec((None, 1, LANE), seq),
                pl.BlockSpec((None, 1, LANE), seq),
                pl.BlockSpec((pg, None, 1, page), lambda s, c, pt: (c, s, 0, 0)),
                pl.BlockSpec((None, 1, LANE), seq)]
    in_specs += [pl.BlockSpec((None, HW, page), page_map(i)) for i in range(pg)]
    in_specs += [pl.BlockSpec((None, HW, page), page_map(i)) for i in range(pg)]
    return pl.pallas_call(
        functools.partial(_dsa_decode_kernel, pg=pg, npages=npages),
        grid_spec=pltpu.PrefetchScalarGridSpec(
            num_scalar_prefetch=1,
            grid=(n, npages // pg),
            in_specs=in_specs,
            out_specs=pl.BlockSpec((None, 1, HW), seq),
            scratch_shapes=[pltpu.VMEM((N_HEADS, LANE), F32), pltpu.VMEM((N_HEADS, LANE), F32),
                            pltpu.VMEM((N_HEADS, HW), F32)]),
        out_shape=jax.ShapeDtypeStruct((n, 1, HW), F32),
        compiler_params=_cparams("arbitrary", "arbitrary"),
        name="dsa_decode",
    )(page_table, q3, kn3, vn3, t3, j3, keys4, knew3, *([cache_kt] * pg), *([cache_vt] * pg))


def kernel(x_prompt, x_sample, cache_fox_k, cache_fox_v, cache_fox_logf, cache_dsa_k, cache_dsa_v, cache_idx_k, cache_mem_k, cache_mem_v, page_table, mem_prompt, w_in, b_fgate, w_fox_up, w_dsa_up, w_mix_out, ln1_g, ln1_b, w_mq, w_mkv, w_mo, ln2_g, ln2_b, w_ffn_gate, w_ffn_up, w_ffn_down, ln3_g, ln3_b):
    assert w_in.shape[0] == 1, "single-layer step"
    b, s, d = x_prompt.shape
    n, t_new, _ = x_sample.shape
    assert t_new == 1
    npages = page_table.shape[1]
    page = cache_fox_k.shape[2]
    past = npages * page
    ksel_p = min(TOPK_MAX, s // 4)
    ksel_s = min(TOPK_MAX, (past + t_new) // 4)
    mem_tokens = mem_prompt.shape[1]
    mw = MEM_HEADS * HEAD_DIM

    fq, fk, fv, fg, dq, dk, dv, iq, ik, iw, ga, gb = jnp.split(w_in[0], IN_OFFSETS, axis=1)
    misc = jnp.concatenate([iw, fg, jnp.zeros((d, LANE - 2 * N_HEADS), F32)], axis=1)
    wcat = jnp.concatenate([fq * QK_SCALE, fk, fv, dq * QK_SCALE, dk, dv, iq, ga, gb, ik, ik, misc],
                           axis=1).astype(BF16)
    bpad = jnp.zeros((1, LANE), F32).at[0, MISC_LOGF:MISC_LOGF + N_HEADS].set(b_fgate[0].astype(F32))
    wf, wd, wm = w_fox_up[0].astype(BF16), w_dsa_up[0].astype(BF16), w_mix_out[0].astype(BF16)
    wq = (w_mq[0] * QK_SCALE).astype(BF16)
    wkv, wo = w_mkv[0].astype(BF16), w_mo[0].astype(BF16)
    wg, wu, wdn = w_ffn_gate[0].astype(BF16), w_ffn_up[0].astype(BF16), w_ffn_down[0].astype(BF16)
    vec = lambda v: v[0].astype(F32).reshape(1, d)
    g1, b1, g2, b2, g3, b3 = map(vec, (ln1_g, ln1_b, ln2_g, ln2_b, ln3_g, ln3_b))

    tm = min(256, s)
    tq = min(256, s)
    xp = x_prompt.reshape(b * s, d)
    (fq_p, fk32, fk16, fv32, fv16, dq_p, dk32, dk16, dv32, dv16, iq_p, ga_p, gb_p, ik32, ik16, misc_p) = _in_proj(
        xp, wcat, bpad, _rope_tables(jnp.arange(s)), tm, s // tm)
    r3 = lambda a: a.reshape(b, s, a.shape[-1])
    crow, ct = _fox_scan(r3(misc_p), tq)
    o_f = _fox_prompt(r3(fq_p), r3(fk16), r3(fv16), crow, ct, tq)
    o_d = _dsa_prompt(r3(iq_p), r3(ik16), r3(misc_p), r3(dq_p), r3(dk16), r3(dv16), tq, ksel_p)
    h1, qm = _merge(o_f.reshape(b * s, HW), o_d.reshape(b * s, HW), ga_p, gb_p, xp, wf, wd, wm, g1, b1, wq, tm)
    mk32, mv32, mk16, mv16 = _mem_kv(mem_prompt.reshape(b * mem_tokens, d), wkv, min(256, b * mem_tokens))
    om = _mem_prompt(qm.reshape(b, s, mw), mk16.reshape(b, mem_tokens, mw), mv16.reshape(b, mem_tokens, mw), tm)
    y_p = _post(h1, om.reshape(b * s, mw), wo, g2, b2, wg, wu, wdn, g3, b3, tm)

    xs = x_sample.reshape(n, d)
    pos_s = jnp.full((n,), past, jnp.int32)
    (fq_s, fk32s, _, fv32s, _, dq_s, dk32s, _, dv32s, _, iq_s, ga_s, gb_s, ik32s, _, misc_s) = _in_proj(
        xs, wcat, bpad, _rope_tables(pos_s), n, 1)
    logf_s = misc_s[:, MISC_LOGF:MISC_LOGF + N_HEADS]
    pg = 8 if npages % 8 == 0 else 1
    paged = lambda cch: jnp.transpose(cch[0], (0, 2, 3, 1)).reshape(cch.shape[1], HW, page)
    lft = jnp.swapaxes(cache_fox_logf[0], 1, 2)
    ikt = jnp.swapaxes(cache_idx_k[0], 1, 2)
    o_fs = _fox_decode(page_table, fq_s.reshape(n, 1, HW), fk32s.reshape(n, 1, HW), fv32s.reshape(n, 1, HW),
                       logf_s.reshape(n, N_HEADS, 1), paged(cache_fox_k), paged(cache_fox_v), lft, pg)
    keys4, knew3 = _dsa_scores(page_table, iq_s.reshape(n, N_HEADS, HEAD_DIM),
                               misc_s[:, MISC_IW:MISC_IW + N_HEADS].reshape(n, N_HEADS, 1),
                               ik32s[:, :HEAD_DIM].reshape(n, 1, HEAD_DIM), ikt, pg)
    t_s, j_s = _dsa_select(keys4.reshape(npages, n, page), knew3.reshape(n, LANE), ksel_s)
    o_ds = _dsa_decode(page_table, dq_s.reshape(n, 1, HW), dk32s.reshape(n, 1, HW), dv32s.reshape(n, 1, HW),
                       t_s.reshape(n, 1, LANE), j_s.reshape(n, 1, LANE), keys4, knew3,
                       paged(cache_dsa_k), paged(cache_dsa_v), pg)
    h1s, qms = _merge(o_fs.reshape(n, HW), o_ds.reshape(n, HW), ga_s, gb_s, xs, wf, wd, wm, g1, b1, wq, n)
    memt = lambda cch: jnp.transpose(cch[0], (0, 2, 3, 1)).reshape(n, mw, mem_tokens)
    oms = _mem_sample(qms.reshape(n, 1, mw), memt(cache_mem_k), memt(cache_mem_v))
    y_s = _post(h1s, oms.reshape(n, mw), wo, g2, b2, wg, wu, wdn, g3, b3, n)

    heads = lambda a, lead: a.reshape(1, *lead, N_HEADS, HEAD_DIM)
    return (y_p.reshape(b, s, d), y_s.reshape(n, 1, d),
            heads(fk32, (b, s)), heads(fv32, (b, s)),
            misc_p[:, MISC_LOGF:MISC_LOGF + N_HEADS].reshape(1, b, s, N_HEADS),
            heads(dk32, (b, s)), heads(dv32, (b, s)), ik32[:, :HEAD_DIM].reshape(1, b, s, HEAD_DIM),
            mk32.reshape(1, b, mem_tokens, MEM_HEADS, HEAD_DIM), mv32.reshape(1, b, mem_tokens, MEM_HEADS, HEAD_DIM),
            heads(fk32s, (n, 1)), heads(fv32s, (n, 1)), logf_s.reshape(1, n, 1, N_HEADS),
            heads(dk32s, (n, 1)), heads(dv32s, (n, 1)), ik32s[:, :HEAD_DIM].reshape(1, n, 1, HEAD_DIM))
```

```python
import functools

import numpy as np
import jax
import jax.numpy as jnp
from jax import lax
from jax.experimental import pallas as pl
from jax.experimental.pallas import tpu as pltpu

F32 = jnp.float32
BF16 = jnp.bfloat16
I32 = jnp.int32

LANE = 128
VMEM_LIMIT_BYTES = 56 * 1024 * 1024

HEAD_DIM = 64
N_HEADS = 8
MEM_HEADS = 4
ROT_HALF = 8
ROPE_THETA = 500000.0
LN_EPS = 1e-5
TOPK_MAX = 256
DEEPNORM_ALPHA = 2.0 ** 0.25
QK_SCALE = HEAD_DIM ** -0.5
IDX_SCORE_SCALE = (HEAD_DIM ** -0.5) * (N_HEADS ** -0.5)
FF_CHUNK = 256
ROW_TILE = 256
Q_TILE = 256
KV_TILE = 512
DECODE_PAGES_PER_STEP = 16
SCORE_PAGES_PER_STEP = 64

NEG = -1e30
INT_MIN = -(2 ** 31)

HW = N_HEADS * HEAD_DIM
IN_WIDTHS = [HW, HW, HW, N_HEADS, HW, HW, HW, HW, HEAD_DIM, N_HEADS, 1024, 1024]
IN_OFFSETS = [int(o) for o in np.cumsum(IN_WIDTHS)[:-1]]

C_FQ, C_FK, C_FV, C_DQ, C_DK, C_DV, C_IQ, C_GA, C_GB, C_IK, C_MISC, C_END = (
    0, 512, 1024, 1536, 2048, 2560, 3072, 3584, 4608, 5632, 5760, 5888)
MISC_IW = 0
MISC_LOGF = 8


def _cparams(*sem):
    return pltpu.CompilerParams(dimension_semantics=sem or None, vmem_limit_bytes=VMEM_LIMIT_BYTES)


def _dot(a, b):
    return jnp.dot(a, b, preferred_element_type=F32)


def _dot_nt(a, b):
    return lax.dot_general(a, b, (((1,), (1,)), ((), ())), preferred_element_type=F32)


def _sigmoid(x):
    return 1.0 / (1.0 + jnp.exp(-x))


def _layer_norm(x, g, b):
    mu = jnp.mean(x, axis=-1, keepdims=True)
    xc = x - mu
    var = jnp.mean(xc * xc, axis=-1, keepdims=True)
    return xc * lax.rsqrt(var + LN_EPS) * g + b


def _pages_per_step(npages, want):
    return max(p for p in range(1, min(npages, want) + 1) if npages % p == 0)


def _half_mask(q2, odd):
    lane = lax.broadcasted_iota(I32, q2.shape, 1)
    keep = (lane >= HEAD_DIM) if odd else (lane < HEAD_DIM)
    return jnp.where(keep, q2.astype(F32), 0.0).astype(BF16)


def _inproj_kernel(x_ref, w_ref, cos_ref, sup_ref, sdn_ref, b_ref,
                   fq_ref, fk32_ref, fk16_ref, fv32_ref, fva_ref, fvb_ref,
                   dq_ref, dk32_ref, dk16_ref, dv32_ref, dva_ref, dvb_ref,
                   iq_ref, ga_ref, gb_ref, ik32_ref, ik16_ref, misc_ref):
    xb = x_ref[...].astype(BF16)
    cosv, sup, sdn = cos_ref[...], sup_ref[...], sdn_ref[...]

    def proj(off, width):
        return _dot(xb, w_ref[:, off:off + width])

    def rope(z):
        return (z * cosv + pltpu.roll(z, LANE - ROT_HALF, 1) * sup
                + pltpu.roll(z, ROT_HALF, 1) * sdn)

    def ones_halves(z):
        even = (lax.broadcasted_iota(I32, z.shape, 1) & HEAD_DIM) == 0
        return jnp.where(even, z, 1.0).astype(BF16), jnp.where(even, 1.0, z).astype(BF16)

    fq_ref[...] = proj(C_FQ, HW).astype(BF16)
    z = proj(C_FK, HW)
    fk32_ref[...] = z
    fk16_ref[...] = z.astype(BF16)
    z = proj(C_FV, HW)
    fv32_ref[...] = z
    fva_ref[...], fvb_ref[...] = ones_halves(z)
    z = proj(C_DQ, HW)
    for g in range(HW // LANE):
        dq_ref[:, g * LANE:(g + 1) * LANE] = rope(z[:, g * LANE:(g + 1) * LANE]).astype(BF16)
    z = proj(C_DK, HW)
    for g in range(HW // LANE):
        r = rope(z[:, g * LANE:(g + 1) * LANE])
        dk32_ref[:, g * LANE:(g + 1) * LANE] = r
        dk16_ref[:, g * LANE:(g + 1) * LANE] = r.astype(BF16)
    z = proj(C_DV, HW)
    dv32_ref[...] = z
    dva_ref[...], dvb_ref[...] = ones_halves(z)
    z = proj(C_IQ, HW)
    for g in range(HW // LANE):
        iq_ref[:, g * LANE:(g + 1) * LANE] = rope(z[:, g * LANE:(g + 1) * LANE]).astype(BF16)
    ga_ref[...] = proj(C_GA, 1024)
    gb_ref[...] = proj(C_GB, 1024)
    r = rope(proj(C_IK, LANE))
    ik32_ref[...] = r
    ik16_ref[...] = r.astype(BF16)
    z = proj(C_MISC, LANE)
    lane = lax.broadcasted_iota(I32, z.shape, 1)
    t = z + b_ref[...]
    logf = -(jnp.maximum(-t, 0.0) + jnp.log1p(jnp.exp(-jnp.abs(t))))
    misc_ref[...] = jnp.where(lane < MISC_LOGF, z * IDX_SCORE_SCALE,
                              jnp.where(lane < MISC_LOGF + N_HEADS, logf, 0.0))


def _in_proj(x2, wcat, bpad, tables, tm, table_blocks):
    n, d = x2.shape
    row = lambda i: (i, 0)
    tab = lambda i: (i % table_blocks, 0)
    const = lambda i: (0, 0)
    outs = [(HW, BF16), (HW, F32), (HW, BF16), (HW, F32), (HW, BF16), (HW, BF16),
            (HW, BF16), (HW, F32), (HW, BF16), (HW, F32), (HW, BF16), (HW, BF16),
            (HW, BF16), (1024, F32), (1024, F32), (LANE, F32), (LANE, BF16), (LANE, F32)]
    return pl.pallas_call(
        _inproj_kernel,
        grid=(n // tm,),
        in_specs=[pl.BlockSpec((tm, d), row),
                  pl.BlockSpec((d, C_END), const, pipeline_mode=pl.Buffered(1)),
                  pl.BlockSpec((tm, LANE), tab), pl.BlockSpec((tm, LANE), tab),
                  pl.BlockSpec((tm, LANE), tab), pl.BlockSpec((1, LANE), const)],
        out_specs=[pl.BlockSpec((tm, w), row) for w, _ in outs],
        out_shape=[jax.ShapeDtypeStruct((n, w), dt) for w, dt in outs],
        compiler_params=_cparams("arbitrary"),
        name="in_proj",
    )(x2, wcat, *tables, bpad)


def _rope_tables(pos):
    p = pos.shape[0]
    inv_freq = ROPE_THETA ** (-jnp.arange(ROT_HALF, dtype=F32) / ROT_HALF)
    ang = pos.astype(F32)[:, None] * inv_freq[None, :]
    cos, sin = jnp.cos(ang), jnp.sin(ang)
    zeros = lambda w: jnp.zeros((p, w), F32)
    cos64 = jnp.concatenate([cos, cos, jnp.ones((p, HEAD_DIM - 2 * ROT_HALF), F32)], axis=1)
    sup64 = jnp.concatenate([-sin, zeros(HEAD_DIM - ROT_HALF)], axis=1)
    sdn64 = jnp.concatenate([zeros(ROT_HALF), sin, zeros(HEAD_DIM - 2 * ROT_HALF)], axis=1)
    two = lambda t: jnp.concatenate([t, t], axis=1)
    return two(cos64), two(sup64), two(sdn64)


def _scan_kernel(m_ref, crow_ref, ct_ref):
    x = m_ref[...]
    s = x.shape[0]
    row = lax.broadcasted_iota(I32, x.shape, 0)
    k = 1
    while k < s:
        x = x + jnp.where(row >= k, pltpu.roll(x, k, 0), 0.0)
        k *= 2
    crow_ref[...] = x
    xt = x.T
    nblk, _, tk = ct_ref.shape
    for j in range(nblk):
        ct_ref[j] = xt[MISC_LOGF:MISC_LOGF + N_HEADS, j * tk:(j + 1) * tk]


def _fox_scan(misc3, tk):
    b, s, _ = misc3.shape
    return pl.pallas_call(
        _scan_kernel,
        grid=(b,),
        in_specs=[pl.BlockSpec((None, s, LANE), lambda i: (i, 0, 0))],
        out_specs=[pl.BlockSpec((None, s, LANE), lambda i: (i, 0, 0)),
                   pl.BlockSpec((None, s // tk, N_HEADS, tk), lambda i: (i, 0, 0, 0))],
        out_shape=[jax.ShapeDtypeStruct((b, s, LANE), F32),
                   jax.ShapeDtypeStruct((b, s // tk, N_HEADS, tk), F32)],
        compiler_params=_cparams("arbitrary"),
        name="fox_scan",
    )(misc3)


def _flash_scratch(tq):
    return [pltpu.VMEM((N_HEADS, tq, LANE), BF16),
            pltpu.VMEM((N_HEADS, tq, LANE), F32),
            pltpu.VMEM((N_HEADS // 2, tq, LANE), F32),
            pltpu.VMEM((N_HEADS // 2, tq, LANE), F32)]


def _flash_start(q_ref, qh_ref, m_ref, l_ref, acc_ref):
    for h in range(N_HEADS):
        g = h // 2
        qh_ref[h] = _half_mask(q_ref[:, g * LANE:(g + 1) * LANE], h % 2)
    m_ref[...] = jnp.full(m_ref.shape, NEG, F32)
    l_ref[...] = jnp.zeros(l_ref.shape, F32)
    acc_ref[...] = jnp.zeros(acc_ref.shape, F32)


def _flash_block(qh_ref, m_ref, l_ref, acc_ref, k_ref, va_ref, vb_ref, r0, tk, bias_fn, mask=None):
    tq = qh_ref.shape[1]
    even = lax.broadcasted_iota(I32, (tq, LANE), 1) < HEAD_DIM
    for g in range(N_HEADS // 2):
        gs = slice(g * LANE, (g + 1) * LANE)
        kb = k_ref[pl.ds(r0, tk), gs]
        vs = (va_ref[pl.ds(r0, tk), gs], vb_ref[pl.ds(r0, tk), gs])
        alphas, pvs = [], []
        for e in range(2):
            h = 2 * g + e
            s = _dot_nt(qh_ref[h], kb)
            s = [s[:, c * LANE:(c + 1) * LANE] + bias_fn(h, c) for c in range(tk // LANE)]
            if mask is not None:
                s = [jnp.where(mask[:, c * LANE:(c + 1) * LANE], sc, NEG) for c, sc in enumerate(s)]
            m_old = m_ref[h]
            m_new = jnp.maximum(m_old, jnp.max(functools.reduce(jnp.maximum, s), axis=1, keepdims=True))
            p = jnp.concatenate([jnp.exp(sc - m_new) for sc in s], axis=1)
            m_ref[h] = m_new
            alphas.append(jnp.exp(m_old - m_new))
            pvs.append(_dot(p.astype(BF16), vs[e]))
        acc_ref[g] = jnp.where(even, alphas[0], alphas[1]) * acc_ref[g] + jnp.where(even, pvs[0], pvs[1])
        l_ref[g] = jnp.where(even, alphas[1], alphas[0]) * l_ref[g] + jnp.where(even, pvs[1], pvs[0])


def _flash_finish(o_ref, l_ref, acc_ref):
    for g in range(N_HEADS // 2):
        l = pltpu.roll(l_ref[g], HEAD_DIM, 1)
        o_ref[:, g * LANE:(g + 1) * LANE] = (acc_ref[g] / l).astype(BF16)


def _fox_prompt_kernel(q_ref, k_ref, va_ref, vb_ref, crow_ref, ct_ref, o_ref,
                       qh_ref, m_ref, l_ref, acc_ref, cq_ref):
    qi = pl.program_id(1)
    tq = q_ref.shape[0]
    tk = ct_ref.shape[2]
    _flash_start(q_ref, qh_ref, m_ref, l_ref, acc_ref)
    for h in range(N_HEADS):
        cq_ref[h] = jnp.broadcast_to(crow_ref[:, MISC_LOGF + h:MISC_LOGF + h + 1], (tq, LANE))

    def block(j, mask):
        r0 = pl.multiple_of(j * tk, tk)
        _flash_block(qh_ref, m_ref, l_ref, acc_ref, k_ref, va_ref, vb_ref, r0, tk,
                     lambda h, c: cq_ref[h] - ct_ref[j, h:h + 1, c * LANE:(c + 1) * LANE], mask)

    def body(j, _):
        block(j, None)
        return 0

    nfull = (qi * tq) // tk
    lax.fori_loop(0, nfull, body, 0)
    causal = (nfull * tk + lax.broadcasted_iota(I32, (tq, tk), 1)
              <= qi * tq + lax.broadcasted_iota(I32, (tq, tk), 0))
    block(nfull, causal)
    _flash_finish(o_ref, l_ref, acc_ref)


def _fox_prompt(q, k, va, vb, crow, ct, tq):
    b, s, _ = q.shape
    nblk, tk = ct.shape[1], ct.shape[3]
    assert tk % tq == 0
    qmap = lambda i, j: (i, j, 0)
    full = lambda i, j: (i, 0, 0)
    return pl.pallas_call(
        _fox_prompt_kernel,
        grid=(b, s // tq),
        in_specs=[pl.BlockSpec((None, tq, HW), qmap),
                  pl.BlockSpec((None, s, HW), full), pl.BlockSpec((None, s, HW), full),
                  pl.BlockSpec((None, s, HW), full),
                  pl.BlockSpec((None, tq, LANE), qmap),
                  pl.BlockSpec((None, nblk, N_HEADS, tk), lambda i, j: (i, 0, 0, 0))],
        out_specs=pl.BlockSpec((None, tq, HW), qmap),
        out_shape=jax.ShapeDtypeStruct((b, s, HW), BF16),
        scratch_shapes=_flash_scratch(tq) + [pltpu.VMEM((N_HEADS, tq, LANE), F32)],
        compiler_params=_cparams("arbitrary", "arbitrary"),
        name="fox_prompt",
    )(q, k, va, vb, crow, ct)


COUNT_ROWS = 128
MAX_COUNT_PREDS = 2


def _key_to_float(key):
    return lax.bitcast_convert_type(jnp.where(key < 0, key ^ 0x7FFFFFFF, key), F32)


def _kth_largest(count, ksel, rows):
    ge = lambda cand: (lambda sc, col, rs, f=_key_to_float(cand): sc >= f[rs])
    (c,) = count([lambda sc, col, rs: sc >= 0.0])
    t0 = jnp.where(c >= ksel, jnp.zeros((rows, LANE), I32), jnp.full((rows, LANE), INT_MIN, I32))

    def body(i, t):
        cand = t | lax.shift_left(jnp.int32(1), 30 - i)
        (c,) = count([ge(cand)])
        return jnp.where(c >= ksel, cand, t)

    t = lax.fori_loop(0, 31, body, t0)
    return jnp.where(t == INT_MIN, -jnp.inf, _key_to_float(t))


def _tie_cut(count, t, ksel, nbits, rows):
    n_gt, n_ge = count([lambda sc, col, rs: sc > t[rs], lambda sc, col, rs: sc >= t[rs]])
    need = ksel - n_gt
    surplus = (n_ge > ksel) & (t > -jnp.inf)
    keep_all = jnp.full((rows, LANE), (1 << nbits) - 1, I32)

    def search():
        def body(i, j):
            cand = j | lax.shift_left(jnp.int32(1), nbits - 1 - i)
            (c,) = count([lambda sc, col, rs: (sc == t[rs]) & (col < cand[rs])])
            return jnp.where(c < need, cand, j)
        return lax.fori_loop(0, nbits, body, jnp.zeros((rows, LANE), I32))

    return lax.cond(jnp.max(surplus.astype(I32)) > 0, search, lambda: keep_all)


def _dsa_prompt_kernel(iq_ref, ik_ref, misc_ref, q_ref, k_ref, va_ref, vb_ref, o_ref,
                       sc_ref, lhs_ref, cnt_ref, qh_ref, m_ref, l_ref, acc_ref, *, ksel, nbits):
    qi = pl.program_id(1)
    tq = q_ref.shape[0]
    kc = sc_ref.shape[2]
    nch = (qi * tq) // kc + 1
    lane = lax.broadcasted_iota(I32, (tq, LANE), 1)
    rows_g = qi * tq + lax.broadcasted_iota(I32, (tq, kc), 0)
    cols_l = lax.broadcasted_iota(I32, (tq, kc), 1)

    for h in range(N_HEADS):
        g = h // 2
        lhs_ref[h] = _half_mask(iq_ref[:, g * LANE:(g + 1) * LANE], h % 2)
    wcols = [misc_ref[:, MISC_IW + h:MISC_IW + h + 1] for h in range(N_HEADS)]

    def score_chunk(j, _):
        ikc = ik_ref[pl.ds(pl.multiple_of(j * kc, kc), kc), :]
        sc = jnp.zeros((tq, kc), F32)
        for h in range(N_HEADS):
            sc = sc + jnp.maximum(_dot_nt(lhs_ref[h], ikc), 0.0) * wcols[h]
        sc_ref[j] = jnp.where(j * kc + cols_l <= rows_g, sc, -jnp.inf)
        return 0

    lax.fori_loop(0, nch, score_chunk, 0)

    rb = min(COUNT_ROWS, tq)
    lane_rb = lax.broadcasted_iota(I32, (rb, LANE), 1)

    def count(preds):
        for r in range(tq // rb):
            rs = slice(r * rb, (r + 1) * rb)

            def body(j, parts, rs=rs):
                parts = list(parts)
                for g in range(kc // LANE):
                    tile = sc_ref[j, rs, g * LANE:(g + 1) * LANE]
                    col = j * kc + g * LANE + lane_rb
                    for i, pred in enumerate(preds):
                        parts[i] = parts[i] + pred(tile, col, rs).astype(I32)
                return tuple(parts)

            parts = lax.fori_loop(0, nch, body, tuple(jnp.zeros((rb, LANE), I32) for _ in preds))
            for i, part in enumerate(parts):
                cnt_ref[i, rs, :] = part
        return [jnp.broadcast_to(jnp.sum(cnt_ref[i], axis=1, keepdims=True), cnt_ref.shape[1:])
                for i in range(len(preds))]

    t = _kth_largest(count, ksel, tq)
    jcut = _tie_cut(count, t, ksel, nbits, tq)

    def bias_chunk(j, _):
        for g in range(kc // LANE):
            gs = slice(g * LANE, (g + 1) * LANE)
            col = j * kc + g * LANE + lane
            tile = sc_ref[j, :, gs]
            sel = ((tile > t) | ((tile == t) & (col <= jcut))) & (col <= rows_g[:, :LANE])
            sc_ref[j, :, gs] = jnp.where(sel, 0.0, NEG)
        return 0

    lax.fori_loop(0, nch, bias_chunk, 0)

    _flash_start(q_ref, qh_ref, m_ref, l_ref, acc_ref)

    def block(j, _):
        _flash_block(qh_ref, m_ref, l_ref, acc_ref, k_ref, va_ref, vb_ref, pl.multiple_of(j * kc, kc), kc,
                     lambda h, c: sc_ref[j, :, c * LANE:(c + 1) * LANE])
        return 0

    lax.fori_loop(0, nch, block, 0)
    _flash_finish(o_ref, l_ref, acc_ref)


def _dsa_prompt(iq, ik2, misc3, q, k, va, vb, tq, tk, ksel):
    b, s, _ = q.shape
    assert tk % tq == 0
    nbits = max(1, int(s - 1).bit_length())
    qmap = lambda i, j: (i, j, 0)
    full = lambda i, j: (i, 0, 0)
    return pl.pallas_call(
        functools.partial(_dsa_prompt_kernel, ksel=ksel, nbits=nbits),
        grid=(b, s // tq),
        in_specs=[pl.BlockSpec((None, tq, HW), qmap),
                  pl.BlockSpec((None, s, LANE), full),
                  pl.BlockSpec((None, tq, LANE), qmap),
                  pl.BlockSpec((None, tq, HW), qmap),
                  pl.BlockSpec((None, s, HW), full),
                  pl.BlockSpec((None, s, HW), full),
                  pl.BlockSpec((None, s, HW), full)],
        out_specs=pl.BlockSpec((None, tq, HW), qmap),
        out_shape=jax.ShapeDtypeStruct((b, s, HW), BF16),
        scratch_shapes=[pltpu.VMEM((s // tk, tq, tk), F32),
                        pltpu.VMEM((N_HEADS, tq, LANE), BF16),
                        pltpu.VMEM((MAX_COUNT_PREDS, tq, LANE), I32)
                        ] + _flash_scratch(tq),
        compiler_params=_cparams("arbitrary", "arbitrary"),
        name="dsa_prompt",
    )(iq, ik2, misc3, q, k, va, vb)


def _merge_kernel(of_ref, od_ref, ga_ref, gb_ref, x_ref, wf_ref, wd_ref, wm_ref, g_ref, b_ref, wq_ref,
                  h_ref, qm_ref):
    a1 = _dot(of_ref[...].astype(BF16), wf_ref[...])
    a2 = _dot(od_ref[...].astype(BF16), wd_ref[...])
    u = _sigmoid(ga_ref[...]) * a1 + _sigmoid(gb_ref[...]) * a2
    a = _dot(u.astype(BF16), wm_ref[...])
    h = _layer_norm(DEEPNORM_ALPHA * x_ref[...] + a, g_ref[...], b_ref[...])
    h_ref[...] = h
    qm_ref[...] = _dot(h.astype(BF16), wq_ref[...]).astype(BF16)


def _merge(of, od, ga, gb, x2, wf, wd, wm, g1, b1, wq, tm):
    n, d = x2.shape
    mw = wq.shape[1]
    row = lambda i: (i, 0)
    const = lambda i: (0, 0)
    wspec = lambda w: pl.BlockSpec(w.shape, const, pipeline_mode=pl.Buffered(1))
    return pl.pallas_call(
        _merge_kernel,
        grid=(n // tm,),
        in_specs=[pl.BlockSpec((tm, HW), row), pl.BlockSpec((tm, HW), row),
                  pl.BlockSpec((tm, d), row), pl.BlockSpec((tm, d), row), pl.BlockSpec((tm, d), row),
                  wspec(wf), wspec(wd), wspec(wm), wspec(g1), wspec(b1), wspec(wq)],
        out_specs=[pl.BlockSpec((tm, d), row), pl.BlockSpec((tm, mw), row)],
        out_shape=[jax.ShapeDtypeStruct((n, d), F32), jax.ShapeDtypeStruct((n, mw), BF16)],
        compiler_params=_cparams("arbitrary"),
        name="merge_ln1",
    )(of, od, ga, gb, x2, wf, wd, wm, g1, b1, wq)


def _memkv_kernel(x_ref, w_ref, k32_ref, v32_ref, k16_ref, v16_ref):
    z = _dot(x_ref[...].astype(BF16), w_ref[...])
    mw = k32_ref.shape[1]
    k32_ref[...] = z[:, :mw]
    v32_ref[...] = z[:, mw:]
    k16_ref[...] = z[:, :mw].astype(BF16)
    v16_ref[...] = z[:, mw:].astype(BF16)


def _mem_kv(mem2, wkv, tm):
    n, d = mem2.shape
    mw = wkv.shape[1] // 2
    row = lambda i: (i, 0)
    return pl.pallas_call(
        _memkv_kernel,
        grid=(n // tm,),
        in_specs=[pl.BlockSpec((tm, d), row), pl.BlockSpec(wkv.shape, lambda i: (0, 0))],
        out_specs=[pl.BlockSpec((tm, mw), row)] * 4,
        out_shape=[jax.ShapeDtypeStruct((n, mw), F32), jax.ShapeDtypeStruct((n, mw), F32),
                   jax.ShapeDtypeStruct((n, mw), BF16), jax.ShapeDtypeStruct((n, mw), BF16)],
        compiler_params=_cparams("arbitrary"),
        name="mem_kv",
    )(mem2, wkv)


def _mem_prompt_kernel(q_ref, k_ref, v_ref, o_ref):
    tm = q_ref.shape[0]
    lane = lax.broadcasted_iota(I32, (tm, LANE), 1)
    for g in range(q_ref.shape[1] // LANE):
        gs = slice(g * LANE, (g + 1) * LANE)
        q2, kb, vb = q_ref[:, gs], k_ref[:, gs], v_ref[:, gs]
        outs = []
        for e in range(2):
            s = _dot_nt(_half_mask(q2, e), kb)
            p = jnp.exp(s - jnp.max(s, axis=1, keepdims=True))
            outs.append(_dot(p.astype(BF16), vb) / jnp.sum(p, axis=1, keepdims=True))
        o_ref[:, gs] = jnp.where(lane < HEAD_DIM, outs[0], outs[1]).astype(BF16)


def _mem_prompt(qm, mk, mv, tm):
    b, s, mw = qm.shape
    m = mk.shape[1]
    return pl.pallas_call(
        _mem_prompt_kernel,
        grid=(b, s // tm),
        in_specs=[pl.BlockSpec((None, tm, mw), lambda i, j: (i, j, 0)),
                  pl.BlockSpec((None, m, mw), lambda i, j: (i, 0, 0)),
                  pl.BlockSpec((None, m, mw), lambda i, j: (i, 0, 0))],
        out_specs=pl.BlockSpec((None, tm, mw), lambda i, j: (i, j, 0)),
        out_shape=jax.ShapeDtypeStruct((b, s, mw), BF16),
        compiler_params=_cparams("arbitrary", "arbitrary"),
        name="mem_attn_prompt",
    )(qm, mk, mv)


def _block_diag(row_vec, nrows):
    w = row_vec.shape[1]
    rows = lax.broadcasted_iota(I32, (nrows, w), 0)
    lanes = lax.broadcasted_iota(I32, (nrows, w), 1)
    blk = rows == lax.shift_right_logical(lanes, 6)
    return jnp.where(blk, jnp.broadcast_to(row_vec.astype(F32), (nrows, w)), 0.0), blk


def _mem_sample_kernel(q_ref, kt_ref, vt_ref, o_ref):
    qbd, blk = _block_diag(q_ref[...], 8)
    s = _dot(qbd.astype(BF16), kt_ref[...].astype(BF16))
    p = jnp.exp(s - jnp.max(s, axis=1, keepdims=True))
    o = _dot_nt(p.astype(BF16), vt_ref[...].astype(BF16)) / jnp.sum(p, axis=1, keepdims=True)
    o_ref[...] = jnp.sum(jnp.where(blk, o, 0.0), axis=0, keepdims=True)


def _mem_sample(qm3, ck, cv):
    n, mw, m = ck.shape
    return pl.pallas_call(
        _mem_sample_kernel,
        grid=(n,),
        in_specs=[pl.BlockSpec((None, 1, mw), lambda i: (i, 0, 0)),
                  pl.BlockSpec((None, mw, m), lambda i: (i, 0, 0)),
                  pl.BlockSpec((None, mw, m), lambda i: (i, 0, 0))],
        out_specs=pl.BlockSpec((None, 1, mw), lambda i: (i, 0, 0)),
        out_shape=jax.ShapeDtypeStruct((n, 1, mw), F32),
        compiler_params=_cparams("arbitrary"),
        name="mem_attn_sample",
    )(qm3, ck, cv)


def _post_kernel(h_ref, om_ref, wo_ref, g2_ref, b2_ref, wg_ref, wu_ref, wdn_ref, g3_ref, b3_ref, y_ref):
    h2 = _layer_norm(DEEPNORM_ALPHA * h_ref[...] + _dot(om_ref[...].astype(BF16), wo_ref[...]),
                     g2_ref[...], b2_ref[...])
    hb = h2.astype(BF16)
    ff = wg_ref.shape[1]
    acc = jnp.zeros(h2.shape, F32)
    for c in range(ff // FF_CHUNK):
        cs = slice(c * FF_CHUNK, (c + 1) * FF_CHUNK)
        gate = _dot(hb, wg_ref[:, cs])
        act = gate * _sigmoid(gate) * _dot(hb, wu_ref[:, cs])
        acc = acc + _dot(act.astype(BF16), wdn_ref[cs, :])
    y_ref[...] = _layer_norm(DEEPNORM_ALPHA * h2 + acc, g3_ref[...], b3_ref[...])


def _post(h1, om, wo, g2, b2, wg, wu, wdn, g3, b3, tm):
    n, d = h1.shape
    row = lambda i: (i, 0)
    const = lambda i: (0, 0)
    wspec = lambda w: pl.BlockSpec(w.shape, const, pipeline_mode=pl.Buffered(1))
    return pl.pallas_call(
        _post_kernel,
        grid=(n // tm,),
        in_specs=[pl.BlockSpec((tm, d), row), pl.BlockSpec((tm, om.shape[1]), row),
                  wspec(wo), wspec(g2), wspec(b2), wspec(wg), wspec(wu), wspec(wdn), wspec(g3), wspec(b3)],
        out_specs=pl.BlockSpec((tm, d), row),
        out_shape=jax.ShapeDtypeStruct((n, d), F32),
        compiler_params=_cparams("arbitrary"),
        name="post_ffn",
    )(h1, om, wo, g2, b2, wg, wu, wdn, g3, b3)


def _pages_update(qbd, kt_refs, vt_refs, biases, m_ref, l_ref, acc_ref):
    s = [_dot(qbd, kt[...].astype(BF16)) + b for kt, b in zip(kt_refs, biases)]
    smax = functools.reduce(jnp.maximum, s)
    m = m_ref[...]
    m_new = jnp.maximum(m, jnp.max(smax, axis=1, keepdims=True))
    alpha = jnp.exp(m - m_new)
    p = [jnp.exp(si - m_new) for si in s]
    l_ref[...] = alpha * l_ref[...] + jnp.sum(functools.reduce(jnp.add, p), axis=1, keepdims=True)
    pv = functools.reduce(jnp.add, [_dot_nt(pi.astype(BF16), vt[...].astype(BF16)) for pi, vt in zip(p, vt_refs)])
    acc_ref[...] = alpha[:, :1] * acc_ref[...] + pv
    m_ref[...] = m_new


def _page_finish(o_ref, blk, l_ref, acc_ref):
    o = acc_ref[...] / l_ref[:, :1]
    o_ref[...] = jnp.sum(jnp.where(blk, o, 0.0), axis=0, keepdims=True)


def _fox_decode_kernel(pt_ref, q_ref, kn_ref, vn_ref, lfn_ref, *rest, pg):
    k_refs, v_refs, lf_refs = rest[:pg], rest[pg:2 * pg], rest[2 * pg:3 * pg]
    o_ref, m_ref, l_ref, acc_ref, suf_ref = rest[3 * pg:]
    c = pl.program_id(1)
    qbd_f, blk = _block_diag(q_ref[...], N_HEADS)
    qbd = qbd_f.astype(BF16)

    @pl.when(c == 0)
    def _():
        kn = kn_ref[...].astype(BF16).astype(F32)
        m_ref[...] = jnp.broadcast_to(jnp.sum(qbd_f * kn, axis=1, keepdims=True), m_ref.shape)
        l_ref[...] = jnp.ones(l_ref.shape, F32)
        acc_ref[...] = jnp.broadcast_to(vn_ref[...].astype(BF16).astype(F32), acc_ref.shape)
        suf_ref[...] = jnp.broadcast_to(lfn_ref[...], suf_ref.shape)

    lane = lax.broadcasted_iota(I32, (N_HEADS, LANE), 1)
    suf = suf_ref[...]
    biases = []
    for i in range(pg):
        y = lf_refs[i][...]
        k = 1
        while k < LANE:
            y = y + jnp.where(lane + k < LANE, pltpu.roll(y, LANE - k, 1), 0.0)
            k *= 2
        biases.append(suf + jnp.where(lane < LANE - 1, pltpu.roll(y, LANE - 1, 1), 0.0))
        suf = suf + y[:, :1]
    suf_ref[...] = suf
    _pages_update(qbd, k_refs, v_refs, biases, m_ref, l_ref, acc_ref)

    @pl.when(c == pl.num_programs(1) - 1)
    def _():
        _page_finish(o_ref, blk, l_ref, acc_ref)


def _fox_decode(page_table, q3, kn3, vn3, lfn3, cache_kt, cache_vt, cache_lft, pg):
    n, npages = page_table.shape
    page = cache_kt.shape[2]
    assert page == LANE and npages % pg == 0

    def page_map(i):
        return lambda s, c, pt: (pt[s, npages - 1 - (c * pg + i)], 0, 0)

    seq = lambda s, c, pt: (s, 0, 0)
    in_specs = [pl.BlockSpec((None, 1, HW), seq), pl.BlockSpec((None, 1, HW), seq),
                pl.BlockSpec((None, 1, HW), seq), pl.BlockSpec((None, N_HEADS, 1), seq)]
    in_specs += [pl.BlockSpec((None, HW, page), page_map(i)) for i in range(pg)]
    in_specs += [pl.BlockSpec((None, HW, page), page_map(i)) for i in range(pg)]
    in_specs += [pl.BlockSpec((None, N_HEADS, page), page_map(i)) for i in range(pg)]
    return pl.pallas_call(
        functools.partial(_fox_decode_kernel, pg=pg),
        grid_spec=pltpu.PrefetchScalarGridSpec(
            num_scalar_prefetch=1,
            grid=(n, npages // pg),
            in_specs=in_specs,
            out_specs=pl.BlockSpec((None, 1, HW), seq),
            scratch_shapes=[pltpu.VMEM((N_HEADS, LANE), F32), pltpu.VMEM((N_HEADS, LANE), F32),
                            pltpu.VMEM((N_HEADS, HW), F32), pltpu.VMEM((N_HEADS, LANE), F32)]),
        out_shape=jax.ShapeDtypeStruct((n, 1, HW), F32),
        compiler_params=_cparams("arbitrary", "arbitrary"),
        name="fox_decode",
    )(page_table, q3, kn3, vn3, lfn3, *([cache_kt] * pg), *([cache_vt] * pg), *([cache_lft] * pg))


def _dsa_score_kernel(pt_ref, iq_ref, w_ref, ikn_ref, *rest, pg):
    pages = rest[:pg]
    keys_ref, knew_ref = rest[pg], rest[pg + 1]
    iq = iq_ref[...]
    w = w_ref[...]
    for i in range(pg):
        r = _dot(iq, pages[i][...].astype(BF16))
        keys_ref[i] = jnp.sum(jnp.maximum(r, 0.0) * w, axis=0, keepdims=True)

    @pl.when(pl.program_id(1) == 0)
    def _():
        ikn = ikn_ref[...].astype(BF16).astype(F32)
        r = jnp.sum(iq.astype(F32) * ikn, axis=1, keepdims=True)
        sc = jnp.sum(jnp.maximum(r, 0.0) * w, axis=0, keepdims=True)
        lane = lax.broadcasted_iota(I32, knew_ref.shape, 1)
        knew_ref[...] = jnp.where(lane == 0, jnp.broadcast_to(sc, knew_ref.shape), -jnp.inf)


def _dsa_scores(page_table, iq3, w3, ikn3, cache_ikt, pg):
    n, npages = page_table.shape
    page = cache_ikt.shape[2]
    assert page == LANE and npages % pg == 0

    def page_map(i):
        return lambda s, c, pt: (pt[s, c * pg + i], 0, 0)

    seq = lambda s, c, pt: (s, 0, 0)
    in_specs = [pl.BlockSpec((None, N_HEADS, HEAD_DIM), seq), pl.BlockSpec((None, N_HEADS, 1), seq),
                pl.BlockSpec((None, 1, HEAD_DIM), seq)]
    in_specs += [pl.BlockSpec((None, HEAD_DIM, page), page_map(i)) for i in range(pg)]
    return pl.pallas_call(
        functools.partial(_dsa_score_kernel, pg=pg),
        grid_spec=pltpu.PrefetchScalarGridSpec(
            num_scalar_prefetch=1,
            grid=(n, npages // pg),
            in_specs=in_specs,
            out_specs=[pl.BlockSpec((pg, None, 1, page), lambda s, c, pt: (c, s, 0, 0)),
                       pl.BlockSpec((None, 1, LANE), seq)]),
        out_shape=[jax.ShapeDtypeStruct((npages, n, 1, page), F32),
                   jax.ShapeDtypeStruct((n, 1, LANE), F32)],
        compiler_params=_cparams("arbitrary", "arbitrary"),
        name="dsa_decode_scores",
    )(page_table, iq3, w3, ikn3, *([cache_ikt] * pg))


def _select_kernel(keys_ref, knew_ref, t_ref, j_ref, cnt_ref, *, ksel, nbits):
    ng, n, _ = keys_ref.shape
    lane = lax.broadcasted_iota(I32, (n, LANE), 1)

    rb = min(COUNT_ROWS, n)
    lane_rb = lax.broadcasted_iota(I32, (rb, LANE), 1)

    def count(preds):
        for r in range(n // rb):
            rs = slice(r * rb, (r + 1) * rb)

            def body(g, parts, rs=rs):
                tile, col = keys_ref[g, rs, :], g * LANE + lane_rb
                return tuple(part + pred(tile, col, rs).astype(I32) for part, pred in zip(parts, preds))

            parts = lax.fori_loop(0, ng, body, tuple(jnp.zeros((rb, LANE), I32) for _ in preds))
            for i, (part, pred) in enumerate(zip(parts, preds)):
                part = part + pred(knew_ref[rs, :], ng * LANE + lane_rb, rs).astype(I32)
                cnt_ref[i, rs, :] = part
        return [jnp.broadcast_to(jnp.sum(cnt_ref[i], axis=1, keepdims=True), cnt_ref.shape[1:])
                for i in range(len(preds))]

    t = _kth_largest(count, ksel, n)
    t_ref[...] = t
    j_ref[...] = _tie_cut(count, t, ksel, nbits, n)


def _dsa_select(keys3, knew2, ksel):
    ng, n, _ = keys3.shape
    nbits = max(1, int((ng + 1) * LANE - 1).bit_length())
    return pl.pallas_call(
        functools.partial(_select_kernel, ksel=ksel, nbits=nbits),
        out_shape=[jax.ShapeDtypeStruct((n, LANE), F32), jax.ShapeDtypeStruct((n, LANE), I32)],
        scratch_shapes=[pltpu.VMEM((MAX_COUNT_PREDS, n, LANE), I32)],
        compiler_params=_cparams(),
        name="dsa_decode_select",
    )(keys3, knew2)


def _dsa_decode_kernel(pt_ref, q_ref, kn_ref, vn_ref, t_ref, j_ref, keys_ref, knew_ref, *rest, pg, npages):
    k_refs, v_refs = rest[:pg], rest[pg:2 * pg]
    o_ref, m_ref, l_ref, acc_ref = rest[2 * pg:]
    c = pl.program_id(1)
    qbd_f, blk = _block_diag(q_ref[...], N_HEADS)
    qbd = qbd_f.astype(BF16)
    t, jcut = t_ref[...], j_ref[...]
    lane = lax.broadcasted_iota(I32, (1, LANE), 1)

    def selected(key, col):
        return (key > t) | ((key == t) & (col <= jcut))

    @pl.when(c == 0)
    def _():
        sel_new = selected(knew_ref[...], npages * LANE + lane) & (lane == 0)
        on = jnp.max(jnp.where(sel_new, 1.0, 0.0), axis=1, keepdims=True)
        kn = kn_ref[...].astype(BF16).astype(F32)
        s_new = jnp.sum(qbd_f * kn, axis=1, keepdims=True)
        m_ref[...] = jnp.broadcast_to(jnp.where(on > 0.0, s_new, NEG), m_ref.shape)
        l_ref[...] = jnp.broadcast_to(on, l_ref.shape)
        acc_ref[...] = jnp.broadcast_to(on * vn_ref[...].astype(BF16).astype(F32), acc_ref.shape)

    biases = []
    for i in range(pg):
        col = (c * pg + i) * LANE + lane
        bias = jnp.where(selected(keys_ref[i], col), 0.0, NEG)
        biases.append(jnp.broadcast_to(bias, (N_HEADS, LANE)))
    _pages_update(qbd, k_refs, v_refs, biases, m_ref, l_ref, acc_ref)

    @pl.when(c == pl.num_programs(1) - 1)
    def _():
        _page_finish(o_ref, blk, l_ref, acc_ref)


def _dsa_decode(page_table, q3, kn3, vn3, t3, j3, keys4, knew3, cache_kt, cache_vt, pg):
    n, npages = page_table.shape
    page = cache_kt.shape[2]
    assert page == LANE and npages % pg == 0

    def page_map(i):
        return lambda s, c, pt: (pt[s, c * pg + i], 0, 0)

    seq = lambda s, c, pt: (s, 0, 0)
    in_specs = [pl.BlockSpec((None, 1, HW), seq), pl.BlockSpec((None, 1, HW), seq),
                pl.BlockSpec((None, 1, HW), seq), pl.BlockSpec((None, 1, LANE), seq),
                pl.BlockSpec((None, 1, LANE), seq),
                pl.BlockSpec((pg, None, 1, page), lambda s, c, pt: (c, s, 0, 0)),
                pl.BlockSpec((None, 1, LANE), seq)]
    in_specs += [pl.BlockSpec((None, HW, page), page_map(i)) for i in range(pg)]
    in_specs += [pl.BlockSpec((None, HW, page), page_map(i)) for i in range(pg)]
    return pl.pallas_call(
        functools.partial(_dsa_decode_kernel, pg=pg, npages=npages),
        grid_spec=pltpu.PrefetchScalarGridSpec(
            num_scalar_prefetch=1,
            grid=(n, npages // pg),
            in_specs=in_specs,
            out_specs=pl.BlockSpec((None, 1, HW), seq),
            scratch_shapes=[pltpu.VMEM((N_HEADS, LANE), F32), pltpu.VMEM((N_HEADS, LANE), F32),
                            pltpu.VMEM((N_HEADS, HW), F32)]),
        out_shape=jax.ShapeDtypeStruct((n, 1, HW), F32),
        compiler_params=_cparams("arbitrary", "arbitrary"),
        name="dsa_decode",
    )(page_table, q3, kn3, vn3, t3, j3, keys4, knew3, *([cache_kt] * pg), *([cache_vt] * pg))


def kernel(x_prompt, x_sample, cache_fox_k, cache_fox_v, cache_fox_logf, cache_dsa_k, cache_dsa_v, cache_idx_k, cache_mem_k, cache_mem_v, page_table, mem_prompt, w_in, b_fgate, w_fox_up, w_dsa_up, w_mix_out, ln1_g, ln1_b, w_mq, w_mkv, w_mo, ln2_g, ln2_b, w_ffn_gate, w_ffn_up, w_ffn_down, ln3_g, ln3_b):
    assert w_in.shape[0] == 1, "single-layer step"
    b, s, d = x_prompt.shape
    n, t_new, _ = x_sample.shape
    assert t_new == 1
    npages = page_table.shape[1]
    page = cache_fox_k.shape[2]
    past = npages * page
    ksel_p = min(TOPK_MAX, s // 4)
    ksel_s = min(TOPK_MAX, (past + t_new) // 4)
    mem_tokens = mem_prompt.shape[1]
    mw = MEM_HEADS * HEAD_DIM

    fq, fk, fv, fg, dq, dk, dv, iq, ik, iw, ga, gb = jnp.split(w_in[0], IN_OFFSETS, axis=1)
    misc = jnp.concatenate([iw, fg, jnp.zeros((d, LANE - 2 * N_HEADS), F32)], axis=1)
    wcat = jnp.concatenate([fq * QK_SCALE, fk, fv, dq * QK_SCALE, dk, dv, iq, ga, gb, ik, ik, misc],
                           axis=1).astype(BF16)
    bpad = jnp.zeros((1, LANE), F32).at[0, MISC_LOGF:MISC_LOGF + N_HEADS].set(b_fgate[0].astype(F32))
    wf, wd, wm = w_fox_up[0].astype(BF16), w_dsa_up[0].astype(BF16), w_mix_out[0].astype(BF16)
    wq = (w_mq[0] * QK_SCALE).astype(BF16)
    wkv, wo = w_mkv[0].astype(BF16), w_mo[0].astype(BF16)
    wg, wu, wdn = w_ffn_gate[0].astype(BF16), w_ffn_up[0].astype(BF16), w_ffn_down[0].astype(BF16)
    vec = lambda v: v[0].astype(F32).reshape(1, d)
    g1, b1, g2, b2, g3, b3 = map(vec, (ln1_g, ln1_b, ln2_g, ln2_b, ln3_g, ln3_b))

    tm = min(ROW_TILE, s)
    tq = min(Q_TILE, s)
    tk = min(KV_TILE, s)
    xp = x_prompt.reshape(b * s, d)
    (fq_p, fk32, fk16, fv32, fva, fvb, dq_p, dk32, dk16, dv32, dva, dvb, iq_p, ga_p, gb_p, ik32, ik16,
     misc_p) = _in_proj(xp, wcat, bpad, _rope_tables(jnp.arange(s)), tm, s // tm)
    r3 = lambda a: a.reshape(b, s, a.shape[-1])
    crow, ct = _fox_scan(r3(misc_p), tk)
    o_f = _fox_prompt(r3(fq_p), r3(fk16), r3(fva), r3(fvb), crow, ct, tq)
    o_d = _dsa_prompt(r3(iq_p), r3(ik16), r3(misc_p), r3(dq_p), r3(dk16), r3(dva), r3(dvb), tq, tk, ksel_p)
    h1, qm = _merge(o_f.reshape(b * s, HW), o_d.reshape(b * s, HW), ga_p, gb_p, xp, wf, wd, wm, g1, b1, wq, tm)
    mk32, mv32, mk16, mv16 = _mem_kv(mem_prompt.reshape(b * mem_tokens, d), wkv, min(256, b * mem_tokens))
    om = _mem_prompt(qm.reshape(b, s, mw), mk16.reshape(b, mem_tokens, mw), mv16.reshape(b, mem_tokens, mw), tm)
    y_p = _post(h1, om.reshape(b * s, mw), wo, g2, b2, wg, wu, wdn, g3, b3, tm)

    xs = x_sample.reshape(n, d)
    pos_s = jnp.full((n,), past, jnp.int32)
    (fq_s, fk32s, _, fv32s, _, _, dq_s, dk32s, _, dv32s, _, _, iq_s, ga_s, gb_s, ik32s, _, misc_s) = _in_proj(
        xs, wcat, bpad, _rope_tables(pos_s), n, 1)
    logf_s = misc_s[:, MISC_LOGF:MISC_LOGF + N_HEADS]
    pg = _pages_per_step(npages, DECODE_PAGES_PER_STEP)
    pg_sc = _pages_per_step(npages, SCORE_PAGES_PER_STEP)
    paged = lambda cch: jnp.transpose(cch[0], (0, 2, 3, 1)).reshape(cch.shape[1], HW, page)
    lft = jnp.swapaxes(cache_fox_logf[0], 1, 2)
    ikt = jnp.swapaxes(cache_idx_k[0], 1, 2)
    o_fs = _fox_decode(page_table, fq_s.reshape(n, 1, HW), fk32s.reshape(n, 1, HW), fv32s.reshape(n, 1, HW),
                       logf_s.reshape(n, N_HEADS, 1), paged(cache_fox_k), paged(cache_fox_v), lft, pg)
    keys4, knew3 = _dsa_scores(page_table, iq_s.reshape(n, N_HEADS, HEAD_DIM),
                               misc_s[:, MISC_IW:MISC_IW + N_HEADS].reshape(n, N_HEADS, 1),
                               ik32s[:, :HEAD_DIM].reshape(n, 1, HEAD_DIM), ikt, pg_sc)
    t_s, j_s = _dsa_select(keys4.reshape(npages, n, page), knew3.reshape(n, LANE), ksel_s)
    o_ds = _dsa_decode(page_table, dq_s.reshape(n, 1, HW), dk32s.reshape(n, 1, HW), dv32s.reshape(n, 1, HW),
                       t_s.reshape(n, 1, LANE), j_s.reshape(n, 1, LANE), keys4, knew3,
                       paged(cache_dsa_k), paged(cache_dsa_v), pg)
    h1s, qms = _merge(o_fs.reshape(n, HW), o_ds.reshape(n, HW), ga_s, gb_s, xs, wf, wd, wm, g1, b1, wq, n)
    memt = lambda cch: jnp.transpose(cch[0], (0, 2, 3, 1)).reshape(n, mw, mem_tokens)
    oms = _mem_sample(qms.reshape(n, 1, mw), memt(cache_mem_k), memt(cache_mem_v))
    y_s = _post(h1s, oms.reshape(n, mw), wo, g2, b2, wg, wu, wdn, g3, b3, n)

    heads = lambda a, lead: a.reshape(1, *lead, N_HEADS, HEAD_DIM)
    return (y_p.reshape(b, s, d), y_s.reshape(n, 1, d),
            heads(fk32, (b, s)), heads(fv32, (b, s)),
            misc_p[:, MISC_LOGF:MISC_LOGF + N_HEADS].reshape(1, b, s, N_HEADS),
            heads(dk32, (b, s)), heads(dv32, (b, s)), ik32[:, :HEAD_DIM].reshape(1, b, s, HEAD_DIM),
            mk32.reshape(1, b, mem_tokens, MEM_HEADS, HEAD_DIM), mv32.reshape(1, b, mem_tokens, MEM_HEADS, HEAD_DIM),
            heads(fk32s, (n, 1)), heads(fv32s, (n, 1)), logf_s.reshape(1, n, 1, N_HEADS),
            heads(dk32s, (n, 1)), heads(dv32s, (n, 1)), ik32s[:, :HEAD_DIM].reshape(1, n, 1, HEAD_DIM))
```

```python
import functools

import numpy as np
import jax
import jax.numpy as jnp
from jax import lax
from jax.experimental import pallas as pl
from jax.experimental.pallas import tpu as pltpu

F32 = jnp.float32
BF16 = jnp.bfloat16
I32 = jnp.int32

LANE = 128
VMEM_LIMIT_BYTES = 56 * 1024 * 1024

HEAD_DIM = 64
N_HEADS = 8
MEM_HEADS = 4
ROT_HALF = 8
ROPE_THETA = 500000.0
LN_EPS = 1e-5
TOPK_MAX = 256
DEEPNORM_ALPHA = 2.0 ** 0.25
QK_SCALE = HEAD_DIM ** -0.5
IDX_SCORE_SCALE = (HEAD_DIM ** -0.5) * (N_HEADS ** -0.5)
FF_CHUNK = 256
ROW_TILE = 256
Q_TILE = 512
KV_TILE = 512
DECODE_PAGES_PER_STEP = 16
SCORE_PAGES_PER_STEP = 64

NEG = -1e30
INT_MIN = -(2 ** 31)

HW = N_HEADS * HEAD_DIM
IN_WIDTHS = [HW, HW, HW, N_HEADS, HW, HW, HW, HW, HEAD_DIM, N_HEADS, 1024, 1024]
IN_OFFSETS = [int(o) for o in np.cumsum(IN_WIDTHS)[:-1]]

C_FQ, C_FK, C_FV, C_DQ, C_DK, C_DV, C_IQ, C_GA, C_GB, C_IK, C_MISC, C_END = (
    0, 512, 1024, 1536, 2048, 2560, 3072, 3584, 4608, 5632, 5760, 5888)
MISC_IW = 0
MISC_LOGF = 8


def _cparams(*sem):
    return pltpu.CompilerParams(dimension_semantics=sem or None, vmem_limit_bytes=VMEM_LIMIT_BYTES)


def _dot(a, b):
    return jnp.dot(a, b, preferred_element_type=F32)


def _dot_nt(a, b):
    return lax.dot_general(a, b, (((1,), (1,)), ((), ())), preferred_element_type=F32)


def _sigmoid(x):
    return 1.0 / (1.0 + jnp.exp(-x))


def _layer_norm(x, g, b):
    mu = jnp.mean(x, axis=-1, keepdims=True)
    xc = x - mu
    var = jnp.mean(xc * xc, axis=-1, keepdims=True)
    return xc * lax.rsqrt(var + LN_EPS) * g + b


def _pages_per_step(npages, want):
    return max(p for p in range(1, min(npages, want) + 1) if npages % p == 0)


def _half_mask(q2, odd):
    lane = lax.broadcasted_iota(I32, q2.shape, 1)
    keep = (lane >= HEAD_DIM) if odd else (lane < HEAD_DIM)
    return jnp.where(keep, q2.astype(F32), 0.0).astype(BF16)


def _inproj_kernel(x_ref, w_ref, cos_ref, sup_ref, sdn_ref, b_ref,
                   fq_ref, fk32_ref, fk16_ref, fv32_ref, fva_ref, fvb_ref,
                   dq_ref, dk32_ref, dk16_ref, dv32_ref, dva_ref, dvb_ref,
                   iq_ref, ga_ref, gb_ref, ik32_ref, ik16_ref, misc_ref):
    xb = x_ref[...].astype(BF16)
    cosv, sup, sdn = cos_ref[...], sup_ref[...], sdn_ref[...]

    def proj(off, width):
        return _dot(xb, w_ref[:, off:off + width])

    def rope(z):
        return (z * cosv + pltpu.roll(z, LANE - ROT_HALF, 1) * sup
                + pltpu.roll(z, ROT_HALF, 1) * sdn)

    def ones_halves(z):
        even = (lax.broadcasted_iota(I32, z.shape, 1) & HEAD_DIM) == 0
        return jnp.where(even, z, 1.0).astype(BF16), jnp.where(even, 1.0, z).astype(BF16)

    fq_ref[...] = proj(C_FQ, HW).astype(BF16)
    z = proj(C_FK, HW)
    fk32_ref[...] = z.T
    fk16_ref[...] = z.astype(BF16)
    z = proj(C_FV, HW)
    fv32_ref[...] = z.T
    fva_ref[...], fvb_ref[...] = ones_halves(z)
    z = proj(C_DQ, HW)
    for g in range(HW // LANE):
        dq_ref[:, g * LANE:(g + 1) * LANE] = rope(z[:, g * LANE:(g + 1) * LANE]).astype(BF16)
    z = proj(C_DK, HW)
    for g in range(HW // LANE):
        r = rope(z[:, g * LANE:(g + 1) * LANE])
        dk32_ref[g * LANE:(g + 1) * LANE, :] = r.T
        dk16_ref[:, g * LANE:(g + 1) * LANE] = r.astype(BF16)
    z = proj(C_DV, HW)
    dv32_ref[...] = z.T
    dva_ref[...], dvb_ref[...] = ones_halves(z)
    z = proj(C_IQ, HW)
    for g in range(HW // LANE):
        iq_ref[:, g * LANE:(g + 1) * LANE] = rope(z[:, g * LANE:(g + 1) * LANE]).astype(BF16)
    ga_ref[...] = proj(C_GA, 1024)
    gb_ref[...] = proj(C_GB, 1024)
    r = rope(proj(C_IK, LANE))
    ik32_ref[...] = r
    ik16_ref[...] = r.astype(BF16)
    z = proj(C_MISC, LANE)
    lane = lax.broadcasted_iota(I32, z.shape, 1)
    t = z + b_ref[...]
    logf = -(jnp.maximum(-t, 0.0) + jnp.log1p(jnp.exp(-jnp.abs(t))))
    misc_ref[...] = jnp.where(lane < MISC_LOGF, z * IDX_SCORE_SCALE,
                              jnp.where(lane < MISC_LOGF + N_HEADS, logf, 0.0))


def _in_proj(x2, wcat, bpad, tables, tm, table_blocks):
    n, d = x2.shape
    seq = table_blocks * tm
    row = lambda i: (i, 0)
    tab = lambda i: (i % table_blocks, 0)
    const = lambda i: (0, 0)
    feature_major = lambda i: (i // table_blocks, 0, i % table_blocks)
    outs = [(HW, BF16), (HW, None), (HW, BF16), (HW, None), (HW, BF16), (HW, BF16),
            (HW, BF16), (HW, None), (HW, BF16), (HW, None), (HW, BF16), (HW, BF16),
            (HW, BF16), (1024, F32), (1024, F32), (LANE, F32), (LANE, BF16), (LANE, F32)]
    return pl.pallas_call(
        _inproj_kernel,
        grid=(n // tm,),
        in_specs=[pl.BlockSpec((tm, d), row),
                  pl.BlockSpec((d, C_END), const, pipeline_mode=pl.Buffered(1)),
                  pl.BlockSpec((tm, LANE), tab), pl.BlockSpec((tm, LANE), tab),
                  pl.BlockSpec((tm, LANE), tab), pl.BlockSpec((1, LANE), const)],
        out_specs=[pl.BlockSpec((None, w, tm), feature_major) if dt is None else pl.BlockSpec((tm, w), row)
                   for w, dt in outs],
        out_shape=[jax.ShapeDtypeStruct((n // seq, w, seq), F32) if dt is None
                   else jax.ShapeDtypeStruct((n, w), dt) for w, dt in outs],
        compiler_params=_cparams("arbitrary"),
        name="in_proj",
    )(x2, wcat, *tables, bpad)


def _rope_tables(pos):
    p = pos.shape[0]
    inv_freq = ROPE_THETA ** (-jnp.arange(ROT_HALF, dtype=F32) / ROT_HALF)
    ang = pos.astype(F32)[:, None] * inv_freq[None, :]
    cos, sin = jnp.cos(ang), jnp.sin(ang)
    zeros = lambda w: jnp.zeros((p, w), F32)
    cos64 = jnp.concatenate([cos, cos, jnp.ones((p, HEAD_DIM - 2 * ROT_HALF), F32)], axis=1)
    sup64 = jnp.concatenate([-sin, zeros(HEAD_DIM - ROT_HALF)], axis=1)
    sdn64 = jnp.concatenate([zeros(ROT_HALF), sin, zeros(HEAD_DIM - 2 * ROT_HALF)], axis=1)
    two = lambda t: jnp.concatenate([t, t], axis=1)
    return two(cos64), two(sup64), two(sdn64)


def _scan_kernel(m_ref, crow_ref, ct_ref):
    x = m_ref[...]
    s = x.shape[0]
    row = lax.broadcasted_iota(I32, x.shape, 0)
    k = 1
    while k < s:
        x = x + jnp.where(row >= k, pltpu.roll(x, k, 0), 0.0)
        k *= 2
    crow_ref[...] = x
    xt = x.T
    nblk, _, tk = ct_ref.shape
    for j in range(nblk):
        ct_ref[j] = xt[MISC_LOGF:MISC_LOGF + N_HEADS, j * tk:(j + 1) * tk]


def _fox_scan(misc3, tk):
    b, s, _ = misc3.shape
    return pl.pallas_call(
        _scan_kernel,
        grid=(b,),
        in_specs=[pl.BlockSpec((None, s, LANE), lambda i: (i, 0, 0))],
        out_specs=[pl.BlockSpec((None, s, LANE), lambda i: (i, 0, 0)),
                   pl.BlockSpec((None, s // tk, N_HEADS, tk), lambda i: (i, 0, 0, 0))],
        out_shape=[jax.ShapeDtypeStruct((b, s, LANE), F32),
                   jax.ShapeDtypeStruct((b, s // tk, N_HEADS, tk), F32)],
        compiler_params=_cparams("arbitrary"),
        name="fox_scan",
    )(misc3)


def _flash_scratch(tq):
    return [pltpu.VMEM((N_HEADS, tq, LANE), BF16),
            pltpu.VMEM((N_HEADS, tq, LANE), F32),
            pltpu.VMEM((N_HEADS // 2, tq, LANE), F32),
            pltpu.VMEM((N_HEADS // 2, tq, LANE), F32)]


def _flash_start(q_ref, qh_ref, m_ref, l_ref, acc_ref):
    for h in range(N_HEADS):
        g = h // 2
        qh_ref[h] = _half_mask(q_ref[:, g * LANE:(g + 1) * LANE], h % 2)
    m_ref[...] = jnp.full(m_ref.shape, NEG, F32)
    l_ref[...] = jnp.zeros(l_ref.shape, F32)
    acc_ref[...] = jnp.zeros(acc_ref.shape, F32)


def _flash_block(qh_ref, m_ref, l_ref, acc_ref, k_ref, va_ref, vb_ref, r0, tk, bias_fn, mask=None):
    tq = qh_ref.shape[1]
    even = lax.broadcasted_iota(I32, (tq, LANE), 1) < HEAD_DIM
    for g in range(N_HEADS // 2):
        gs = slice(g * LANE, (g + 1) * LANE)
        kb = k_ref[pl.ds(r0, tk), gs]
        vs = (va_ref[pl.ds(r0, tk), gs], vb_ref[pl.ds(r0, tk), gs])
        alphas, pvs = [], []
        for e in range(2):
            h = 2 * g + e
            s = _dot_nt(qh_ref[h], kb)
            s = [s[:, c * LANE:(c + 1) * LANE] + bias_fn(h, c) for c in range(tk // LANE)]
            if mask is not None:
                s = [jnp.where(mask[:, c * LANE:(c + 1) * LANE], sc, NEG) for c, sc in enumerate(s)]
            m_old = m_ref[h]
            m_new = jnp.maximum(m_old, jnp.max(functools.reduce(jnp.maximum, s), axis=1, keepdims=True))
            p = jnp.concatenate([jnp.exp(sc - m_new) for sc in s], axis=1)
            m_ref[h] = m_new
            alphas.append(jnp.exp(m_old - m_new))
            pvs.append(_dot(p.astype(BF16), vs[e]))
        acc_ref[g] = jnp.where(even, alphas[0], alphas[1]) * acc_ref[g] + jnp.where(even, pvs[0], pvs[1])
        l_ref[g] = jnp.where(even, alphas[1], alphas[0]) * l_ref[g] + jnp.where(even, pvs[1], pvs[0])


def _flash_finish(o_ref, l_ref, acc_ref):
    for g in range(N_HEADS // 2):
        l = pltpu.roll(l_ref[g], HEAD_DIM, 1)
        o_ref[:, g * LANE:(g + 1) * LANE] = (acc_ref[g] / l).astype(BF16)


def _fox_prompt_kernel(q_ref, k_ref, va_ref, vb_ref, crow_ref, ct_ref, o_ref,
                       qh_ref, m_ref, l_ref, acc_ref, cq_ref):
    qi = pl.program_id(1)
    tq = q_ref.shape[0]
    tk = ct_ref.shape[2]
    _flash_start(q_ref, qh_ref, m_ref, l_ref, acc_ref)
    for h in range(N_HEADS):
        cq_ref[h] = jnp.broadcast_to(crow_ref[:, MISC_LOGF + h:MISC_LOGF + h + 1], (tq, LANE))

    def block(j, mask):
        r0 = pl.multiple_of(j * tk, tk)
        _flash_block(qh_ref, m_ref, l_ref, acc_ref, k_ref, va_ref, vb_ref, r0, tk,
                     lambda h, c: cq_ref[h] - ct_ref[j, h:h + 1, c * LANE:(c + 1) * LANE], mask)

    def body(j, _):
        block(j, None)
        return 0

    nfull = (qi * tq) // tk
    lax.fori_loop(0, nfull, body, 0)
    causal = (nfull * tk + lax.broadcasted_iota(I32, (tq, tk), 1)
              <= qi * tq + lax.broadcasted_iota(I32, (tq, tk), 0))
    block(nfull, causal)
    _flash_finish(o_ref, l_ref, acc_ref)


def _fox_prompt(q, k, va, vb, crow, ct, tq):
    b, s, _ = q.shape
    nblk, tk = ct.shape[1], ct.shape[3]
    assert tk % tq == 0
    qmap = lambda i, j: (i, j, 0)
    full = lambda i, j: (i, 0, 0)
    return pl.pallas_call(
        _fox_prompt_kernel,
        grid=(b, s // tq),
        in_specs=[pl.BlockSpec((None, tq, HW), qmap),
                  pl.BlockSpec((None, s, HW), full), pl.BlockSpec((None, s, HW), full),
                  pl.BlockSpec((None, s, HW), full),
                  pl.BlockSpec((None, tq, LANE), qmap),
                  pl.BlockSpec((None, nblk, N_HEADS, tk), lambda i, j: (i, 0, 0, 0))],
        out_specs=pl.BlockSpec((None, tq, HW), qmap),
        out_shape=jax.ShapeDtypeStruct((b, s, HW), BF16),
        scratch_shapes=_flash_scratch(tq) + [pltpu.VMEM((N_HEADS, tq, LANE), F32)],
        compiler_params=_cparams("arbitrary", "arbitrary"),
        name="fox_prompt",
    )(q, k, va, vb, crow, ct)


COUNT_ROWS = 128
MAX_COUNT_PREDS = 2


def _key_to_float(key):
    return lax.bitcast_convert_type(jnp.where(key < 0, key ^ 0x7FFFFFFF, key), F32)


def _kth_largest(count, ksel, rows):
    ge = lambda cand: (lambda sc, col, rs, f=_key_to_float(cand): sc >= f[rs])
    (c,) = count([lambda sc, col, rs: sc >= 0.0])
    t0 = jnp.where(c >= ksel, jnp.zeros((rows, LANE), I32), jnp.full((rows, LANE), INT_MIN, I32))

    def body(i, t):
        cand = t | lax.shift_left(jnp.int32(1), 30 - i)
        (c,) = count([ge(cand)])
        return jnp.where(c >= ksel, cand, t)

    t = lax.fori_loop(0, 31, body, t0)
    return jnp.where(t == INT_MIN, -jnp.inf, _key_to_float(t))


def _tie_cut(count, t, ksel, nbits, rows):
    n_gt, n_ge = count([lambda sc, col, rs: sc > t[rs], lambda sc, col, rs: sc >= t[rs]])
    need = ksel - n_gt
    surplus = (n_ge > ksel) & (t > -jnp.inf)
    keep_all = jnp.full((rows, LANE), (1 << nbits) - 1, I32)

    def search():
        def body(i, j):
            cand = j | lax.shift_left(jnp.int32(1), nbits - 1 - i)
            (c,) = count([lambda sc, col, rs: (sc == t[rs]) & (col < cand[rs])])
            return jnp.where(c < need, cand, j)
        return lax.fori_loop(0, nbits, body, jnp.zeros((rows, LANE), I32))

    return lax.cond(jnp.max(surplus.astype(I32)) > 0, search, lambda: keep_all)


def _dsa_prompt_kernel(iq_ref, ik_ref, misc_ref, q_ref, k_ref, va_ref, vb_ref, o_ref,
                       sc_ref, lhs_ref, cnt_ref, qh_ref, m_ref, l_ref, acc_ref, *, ksel, nbits):
    qi = pl.program_id(1)
    tq = q_ref.shape[0]
    kc = sc_ref.shape[2]
    nch = (qi * tq) // kc + 1
    lane = lax.broadcasted_iota(I32, (tq, LANE), 1)
    rows_g = qi * tq + lax.broadcasted_iota(I32, (tq, kc), 0)
    cols_l = lax.broadcasted_iota(I32, (tq, kc), 1)

    for h in range(N_HEADS):
        g = h // 2
        lhs_ref[h] = _half_mask(iq_ref[:, g * LANE:(g + 1) * LANE], h % 2)
    wcols = [misc_ref[:, MISC_IW + h:MISC_IW + h + 1] for h in range(N_HEADS)]

    def score_chunk(j, _):
        ikc = ik_ref[pl.ds(pl.multiple_of(j * kc, kc), kc), :]
        sc = jnp.zeros((tq, kc), F32)
        for h in range(N_HEADS):
            sc = sc + jnp.maximum(_dot_nt(lhs_ref[h], ikc), 0.0) * wcols[h]
        sc_ref[j] = jnp.where(j * kc + cols_l <= rows_g, sc, -jnp.inf)
        return 0

    lax.fori_loop(0, nch, score_chunk, 0)

    rb = min(COUNT_ROWS, tq)
    lane_rb = lax.broadcasted_iota(I32, (rb, LANE), 1)

    def count(preds):
        for r in range(tq // rb):
            rs = slice(r * rb, (r + 1) * rb)

            def body(j, parts, rs=rs):
                parts = list(parts)
                for g in range(kc // LANE):
                    tile = sc_ref[j, rs, g * LANE:(g + 1) * LANE]
                    col = j * kc + g * LANE + lane_rb
                    for i, pred in enumerate(preds):
                        parts[i] = parts[i] + pred(tile, col, rs).astype(I32)
                return tuple(parts)

            parts = lax.fori_loop(0, nch, body, tuple(jnp.zeros((rb, LANE), I32) for _ in preds))
            for i, part in enumerate(parts):
                cnt_ref[i, rs, :] = part
        return [jnp.broadcast_to(jnp.sum(cnt_ref[i], axis=1, keepdims=True), cnt_ref.shape[1:])
                for i in range(len(preds))]

    t = _kth_largest(count, ksel, tq)
    jcut = _tie_cut(count, t, ksel, nbits, tq)

    def bias_chunk(j, _):
        for g in range(kc // LANE):
            gs = slice(g * LANE, (g + 1) * LANE)
            col = j * kc + g * LANE + lane
            tile = sc_ref[j, :, gs]
            sel = ((tile > t) | ((tile == t) & (col <= jcut))) & (col <= rows_g[:, :LANE])
            sc_ref[j, :, gs] = jnp.where(sel, 0.0, NEG)
        return 0

    lax.fori_loop(0, nch, bias_chunk, 0)

    _flash_start(q_ref, qh_ref, m_ref, l_ref, acc_ref)

    def block(j, _):
        _flash_block(qh_ref, m_ref, l_ref, acc_ref, k_ref, va_ref, vb_ref, pl.multiple_of(j * kc, kc), kc,
                     lambda h, c: sc_ref[j, :, c * LANE:(c + 1) * LANE])
        return 0

    lax.fori_loop(0, nch, block, 0)
    _flash_finish(o_ref, l_ref, acc_ref)


def _dsa_prompt(iq, ik2, misc3, q, k, va, vb, tq, tk, ksel):
    b, s, _ = q.shape
    assert tk % tq == 0
    nbits = max(1, int(s - 1).bit_length())
    qmap = lambda i, j: (i, j, 0)
    full = lambda i, j: (i, 0, 0)
    return pl.pallas_call(
        functools.partial(_dsa_prompt_kernel, ksel=ksel, nbits=nbits),
        grid=(b, s // tq),
        in_specs=[pl.BlockSpec((None, tq, HW), qmap),
                  pl.BlockSpec((None, s, LANE), full),
                  pl.BlockSpec((None, tq, LANE), qmap),
                  pl.BlockSpec((None, tq, HW), qmap),
                  pl.BlockSpec((None, s, HW), full),
                  pl.BlockSpec((None, s, HW), full),
                  pl.BlockSpec((None, s, HW), full)],
        out_specs=pl.BlockSpec((None, tq, HW), qmap),
        out_shape=jax.ShapeDtypeStruct((b, s, HW), BF16),
        scratch_shapes=[pltpu.VMEM((s // tk, tq, tk), F32),
                        pltpu.VMEM((N_HEADS, tq, LANE), BF16),
                        pltpu.VMEM((MAX_COUNT_PREDS, tq, LANE), I32)
                        ] + _flash_scratch(tq),
        compiler_params=_cparams("arbitrary", "arbitrary"),
        name="dsa_prompt",
    )(iq, ik2, misc3, q, k, va, vb)


def _merge_kernel(of_ref, od_ref, ga_ref, gb_ref, x_ref, wf_ref, wd_ref, wm_ref, g_ref, b_ref, wq_ref,
                  h_ref, qm_ref):
    a1 = _dot(of_ref[...].astype(BF16), wf_ref[...])
    a2 = _dot(od_ref[...].astype(BF16), wd_ref[...])
    u = _sigmoid(ga_ref[...]) * a1 + _sigmoid(gb_ref[...]) * a2
    a = _dot(u.astype(BF16), wm_ref[...])
    h = _layer_norm(DEEPNORM_ALPHA * x_ref[...] + a, g_ref[...], b_ref[...])
    h_ref[...] = h
    qm_ref[...] = _dot(h.astype(BF16), wq_ref[...]).astype(BF16)


def _merge(of, od, ga, gb, x2, wf, wd, wm, g1, b1, wq, tm):
    n, d = x2.shape
    mw = wq.shape[1]
    row = lambda i: (i, 0)
    const = lambda i: (0, 0)
    wspec = lambda w: pl.BlockSpec(w.shape, const, pipeline_mode=pl.Buffered(1))
    return pl.pallas_call(
        _merge_kernel,
        grid=(n // tm,),
        in_specs=[pl.BlockSpec((tm, HW), row), pl.BlockSpec((tm, HW), row),
                  pl.BlockSpec((tm, d), row), pl.BlockSpec((tm, d), row), pl.BlockSpec((tm, d), row),
                  wspec(wf), wspec(wd), wspec(wm), wspec(g1), wspec(b1), wspec(wq)],
        out_specs=[pl.BlockSpec((tm, d), row), pl.BlockSpec((tm, mw), row)],
        out_shape=[jax.ShapeDtypeStruct((n, d), F32), jax.ShapeDtypeStruct((n, mw), BF16)],
        compiler_params=_cparams("arbitrary"),
        name="merge_ln1",
    )(of, od, ga, gb, x2, wf, wd, wm, g1, b1, wq)


def _memkv_kernel(x_ref, w_ref, k32_ref, v32_ref, k16_ref, v16_ref):
    z = _dot(x_ref[...].astype(BF16), w_ref[...])
    mw = k32_ref.shape[1]
    k32_ref[...] = z[:, :mw]
    v32_ref[...] = z[:, mw:]
    k16_ref[...] = z[:, :mw].astype(BF16)
    v16_ref[...] = z[:, mw:].astype(BF16)


def _mem_kv(mem2, wkv, tm):
    n, d = mem2.shape
    mw = wkv.shape[1] // 2
    row = lambda i: (i, 0)
    return pl.pallas_call(
        _memkv_kernel,
        grid=(n // tm,),
        in_specs=[pl.BlockSpec((tm, d), row), pl.BlockSpec(wkv.shape, lambda i: (0, 0))],
        out_specs=[pl.BlockSpec((tm, mw), row)] * 4,
        out_shape=[jax.ShapeDtypeStruct((n, mw), F32), jax.ShapeDtypeStruct((n, mw), F32),
                   jax.ShapeDtypeStruct((n, mw), BF16), jax.ShapeDtypeStruct((n, mw), BF16)],
        compiler_params=_cparams("arbitrary"),
        name="mem_kv",
    )(mem2, wkv)


def _mem_prompt_kernel(q_ref, k_ref, v_ref, o_ref):
    tm = q_ref.shape[0]
    lane = lax.broadcasted_iota(I32, (tm, LANE), 1)
    for g in range(q_ref.shape[1] // LANE):
        gs = slice(g * LANE, (g + 1) * LANE)
        q2, kb, vb = q_ref[:, gs], k_ref[:, gs], v_ref[:, gs]
        outs = []
        for e in range(2):
            s = _dot_nt(_half_mask(q2, e), kb)
            p = jnp.exp(s - jnp.max(s, axis=1, keepdims=True))
            outs.append(_dot(p.astype(BF16), vb) / jnp.sum(p, axis=1, keepdims=True))
        o_ref[:, gs] = jnp.where(lane < HEAD_DIM, outs[0], outs[1]).astype(BF16)


def _mem_prompt(qm, mk, mv, tm):
    b, s, mw = qm.shape
    m = mk.shape[1]
    return pl.pallas_call(
        _mem_prompt_kernel,
        grid=(b, s // tm),
        in_specs=[pl.BlockSpec((None, tm, mw), lambda i, j: (i, j, 0)),
                  pl.BlockSpec((None, m, mw), lambda i, j: (i, 0, 0)),
                  pl.BlockSpec((None, m, mw), lambda i, j: (i, 0, 0))],
        out_specs=pl.BlockSpec((None, tm, mw), lambda i, j: (i, j, 0)),
        out_shape=jax.ShapeDtypeStruct((b, s, mw), BF16),
        compiler_params=_cparams("arbitrary", "arbitrary"),
        name="mem_attn_prompt",
    )(qm, mk, mv)


def _block_diag(row_vec, nrows):
    w = row_vec.shape[1]
    rows = lax.broadcasted_iota(I32, (nrows, w), 0)
    lanes = lax.broadcasted_iota(I32, (nrows, w), 1)
    blk = rows == lax.shift_right_logical(lanes, 6)
    return jnp.where(blk, jnp.broadcast_to(row_vec.astype(F32), (nrows, w)), 0.0), blk


def _mem_sample_kernel(q_ref, kt_ref, vt_ref, o_ref):
    qbd, blk = _block_diag(q_ref[...], 8)
    s = _dot(qbd.astype(BF16), kt_ref[...].astype(BF16))
    p = jnp.exp(s - jnp.max(s, axis=1, keepdims=True))
    o = _dot_nt(p.astype(BF16), vt_ref[...].astype(BF16)) / jnp.sum(p, axis=1, keepdims=True)
    o_ref[...] = jnp.sum(jnp.where(blk, o, 0.0), axis=0, keepdims=True)


def _mem_sample(qm3, ck, cv):
    n, mw, m = ck.shape
    return pl.pallas_call(
        _mem_sample_kernel,
        grid=(n,),
        in_specs=[pl.BlockSpec((None, 1, mw), lambda i: (i, 0, 0)),
                  pl.BlockSpec((None, mw, m), lambda i: (i, 0, 0)),
                  pl.BlockSpec((None, mw, m), lambda i: (i, 0, 0))],
        out_specs=pl.BlockSpec((None, 1, mw), lambda i: (i, 0, 0)),
        out_shape=jax.ShapeDtypeStruct((n, 1, mw), F32),
        compiler_params=_cparams("arbitrary"),
        name="mem_attn_sample",
    )(qm3, ck, cv)


def _post_kernel(h_ref, om_ref, wo_ref, g2_ref, b2_ref, wg_ref, wu_ref, wdn_ref, g3_ref, b3_ref, y_ref):
    h2 = _layer_norm(DEEPNORM_ALPHA * h_ref[...] + _dot(om_ref[...].astype(BF16), wo_ref[...]),
                     g2_ref[...], b2_ref[...])
    hb = h2.astype(BF16)
    ff = wg_ref.shape[1]
    acc = jnp.zeros(h2.shape, F32)
    for c in range(ff // FF_CHUNK):
        cs = slice(c * FF_CHUNK, (c + 1) * FF_CHUNK)
        gate = _dot(hb, wg_ref[:, cs])
        act = gate * _sigmoid(gate) * _dot(hb, wu_ref[:, cs])
        acc = acc + _dot(act.astype(BF16), wdn_ref[cs, :])
    y_ref[...] = _layer_norm(DEEPNORM_ALPHA * h2 + acc, g3_ref[...], b3_ref[...])


def _post(h1, om, wo, g2, b2, wg, wu, wdn, g3, b3, tm):
    n, d = h1.shape
    row = lambda i: (i, 0)
    const = lambda i: (0, 0)
    wspec = lambda w: pl.BlockSpec(w.shape, const, pipeline_mode=pl.Buffered(1))
    return pl.pallas_call(
        _post_kernel,
        grid=(n // tm,),
        in_specs=[pl.BlockSpec((tm, d), row), pl.BlockSpec((tm, om.shape[1]), row),
                  wspec(wo), wspec(g2), wspec(b2), wspec(wg), wspec(wu), wspec(wdn), wspec(g3), wspec(b3)],
        out_specs=pl.BlockSpec((tm, d), row),
        out_shape=jax.ShapeDtypeStruct((n, d), F32),
        compiler_params=_cparams("arbitrary"),
        name="post_ffn",
    )(h1, om, wo, g2, b2, wg, wu, wdn, g3, b3)


def _pages_update(qbd, kt_refs, vt_refs, biases, m_ref, l_ref, acc_ref):
    s = [_dot(qbd, kt[...].astype(BF16)) + b for kt, b in zip(kt_refs, biases)]
    smax = functools.reduce(jnp.maximum, s)
    m = m_ref[...]
    m_new = jnp.maximum(m, jnp.max(smax, axis=1, keepdims=True))
    alpha = jnp.exp(m - m_new)
    p = [jnp.exp(si - m_new) for si in s]
    l_ref[...] = alpha * l_ref[...] + jnp.sum(functools.reduce(jnp.add, p), axis=1, keepdims=True)
    pv = functools.reduce(jnp.add, [_dot_nt(pi.astype(BF16), vt[...].astype(BF16)) for pi, vt in zip(p, vt_refs)])
    acc_ref[...] = alpha[:, :1] * acc_ref[...] + pv
    m_ref[...] = m_new


def _page_finish(o_ref, blk, l_ref, acc_ref):
    o = acc_ref[...] / l_ref[:, :1]
    o_ref[...] = jnp.sum(jnp.where(blk, o, 0.0), axis=0, keepdims=True)


def _fox_decode_kernel(pt_ref, q_ref, kn_ref, vn_ref, lfn_ref, *rest, pg):
    k_refs, v_refs, lf_refs = rest[:pg], rest[pg:2 * pg], rest[2 * pg:3 * pg]
    o_ref, m_ref, l_ref, acc_ref, suf_ref = rest[3 * pg:]
    c = pl.program_id(1)
    qbd_f, blk = _block_diag(q_ref[...], N_HEADS)
    qbd = qbd_f.astype(BF16)

    @pl.when(c == 0)
    def _():
        kn = kn_ref[...].astype(BF16).astype(F32)
        m_ref[...] = jnp.broadcast_to(jnp.sum(qbd_f * kn, axis=1, keepdims=True), m_ref.shape)
        l_ref[...] = jnp.ones(l_ref.shape, F32)
        acc_ref[...] = jnp.broadcast_to(vn_ref[...].astype(BF16).astype(F32), acc_ref.shape)
        suf_ref[...] = jnp.broadcast_to(lfn_ref[...], suf_ref.shape)

    lane = lax.broadcasted_iota(I32, (N_HEADS, LANE), 1)
    suf = suf_ref[...]
    biases = []
    for i in range(pg):
        y = lf_refs[i][...]
        k = 1
        while k < LANE:
            y = y + jnp.where(lane + k < LANE, pltpu.roll(y, LANE - k, 1), 0.0)
            k *= 2
        biases.append(suf + jnp.where(lane < LANE - 1, pltpu.roll(y, LANE - 1, 1), 0.0))
        suf = suf + y[:, :1]
    suf_ref[...] = suf
    _pages_update(qbd, k_refs, v_refs, biases, m_ref, l_ref, acc_ref)

    @pl.when(c == pl.num_programs(1) - 1)
    def _():
        _page_finish(o_ref, blk, l_ref, acc_ref)


def _fox_decode(page_table, q3, kn3, vn3, lfn3, cache_kt, cache_vt, cache_lft, pg):
    n, npages = page_table.shape
    page = cache_kt.shape[2]
    assert page == LANE and npages % pg == 0

    def page_map(i):
        return lambda s, c, pt: (pt[s, npages - 1 - (c * pg + i)], 0, 0)

    seq = lambda s, c, pt: (s, 0, 0)
    in_specs = [pl.BlockSpec((None, 1, HW), seq), pl.BlockSpec((None, 1, HW), seq),
                pl.BlockSpec((None, 1, HW), seq), pl.BlockSpec((None, N_HEADS, 1), seq)]
    in_specs += [pl.BlockSpec((None, HW, page), page_map(i)) for i in range(pg)]
    in_specs += [pl.BlockSpec((None, HW, page), page_map(i)) for i in range(pg)]
    in_specs += [pl.BlockSpec((None, N_HEADS, page), page_map(i)) for i in range(pg)]
    return pl.pallas_call(
        functools.partial(_fox_decode_kernel, pg=pg),
        grid_spec=pltpu.PrefetchScalarGridSpec(
            num_scalar_prefetch=1,
            grid=(n, npages // pg),
            in_specs=in_specs,
            out_specs=pl.BlockSpec((None, 1, HW), seq),
            scratch_shapes=[pltpu.VMEM((N_HEADS, LANE), F32), pltpu.VMEM((N_HEADS, LANE), F32),
                            pltpu.VMEM((N_HEADS, HW), F32), pltpu.VMEM((N_HEADS, LANE), F32)]),
        out_shape=jax.ShapeDtypeStruct((n, 1, HW), F32),
        compiler_params=_cparams("arbitrary", "arbitrary"),
        name="fox_decode",
    )(page_table, q3, kn3, vn3, lfn3, *([cache_kt] * pg), *([cache_vt] * pg), *([cache_lft] * pg))


def _dsa_score_kernel(pt_ref, iq_ref, w_ref, ikn_ref, *rest, pg):
    pages = rest[:pg]
    keys_ref, knew_ref = rest[pg], rest[pg + 1]
    iq = iq_ref[...]
    w = w_ref[...]
    for i in range(pg):
        r = _dot(iq, pages[i][...].astype(BF16))
        keys_ref[i] = jnp.sum(jnp.maximum(r, 0.0) * w, axis=0, keepdims=True)

    @pl.when(pl.program_id(1) == 0)
    def _():
        ikn = ikn_ref[...].astype(BF16).astype(F32)
        r = jnp.sum(iq.astype(F32) * ikn, axis=1, keepdims=True)
        sc = jnp.sum(jnp.maximum(r, 0.0) * w, axis=0, keepdims=True)
        lane = lax.broadcasted_iota(I32, knew_ref.shape, 1)
        knew_ref[...] = jnp.where(lane == 0, jnp.broadcast_to(sc, knew_ref.shape), -jnp.inf)


def _dsa_scores(page_table, iq3, w3, ikn3, cache_ikt, pg):
    n, npages = page_table.shape
    page = cache_ikt.shape[2]
    assert page == LANE and npages % pg == 0

    def page_map(i):
        return lambda s, c, pt: (pt[s, c * pg + i], 0, 0)

    seq = lambda s, c, pt: (s, 0, 0)
    in_specs = [pl.BlockSpec((None, N_HEADS, HEAD_DIM), seq), pl.BlockSpec((None, N_HEADS, 1), seq),
                pl.BlockSpec((None, 1, HEAD_DIM), seq)]
    in_specs += [pl.BlockSpec((None, HEAD_DIM, page), page_map(i)) for i in range(pg)]
    return pl.pallas_call(
        functools.partial(_dsa_score_kernel, pg=pg),
        grid_spec=pltpu.PrefetchScalarGridSpec(
            num_scalar_prefetch=1,
            grid=(n, npages // pg),
            in_specs=in_specs,
            out_specs=[pl.BlockSpec((pg, None, 1, page), lambda s, c, pt: (c, s, 0, 0)),
                       pl.BlockSpec((None, 1, LANE), seq)]),
        out_shape=[jax.ShapeDtypeStruct((npages, n, 1, page), F32),
                   jax.ShapeDtypeStruct((n, 1, LANE), F32)],
        compiler_params=_cparams("arbitrary", "arbitrary"),
        name="dsa_decode_scores",
    )(page_table, iq3, w3, ikn3, *([cache_ikt] * pg))


def _select_kernel(keys_ref, knew_ref, t_ref, j_ref, cnt_ref, *, ksel, nbits):
    ng, n, _ = keys_ref.shape
    lane = lax.broadcasted_iota(I32, (n, LANE), 1)

    rb = min(COUNT_ROWS, n)
    lane_rb = lax.broadcasted_iota(I32, (rb, LANE), 1)

    def count(preds):
        for r in range(n // rb):
            rs = slice(r * rb, (r + 1) * rb)

            def body(g, parts, rs=rs):
                tile, col = keys_ref[g, rs, :], g * LANE + lane_rb
                return tuple(part + pred(tile, col, rs).astype(I32) for part, pred in zip(parts, preds))

            parts = lax.fori_loop(0, ng, body, tuple(jnp.zeros((rb, LANE), I32) for _ in preds))
            for i, (part, pred) in enumerate(zip(parts, preds)):
                part = part + pred(knew_ref[rs, :], ng * LANE + lane_rb, rs).astype(I32)
                cnt_ref[i, rs, :] = part
        return [jnp.broadcast_to(jnp.sum(cnt_ref[i], axis=1, keepdims=True), cnt_ref.shape[1:])
                for i in range(len(preds))]

    t = _kth_largest(count, ksel, n)
    t_ref[...] = t
    j_ref[...] = _tie_cut(count, t, ksel, nbits, n)


def _dsa_select(keys3, knew2, ksel):
    ng, n, _ = keys3.shape
    nbits = max(1, int((ng + 1) * LANE - 1).bit_length())
    return pl.pallas_call(
        functools.partial(_select_kernel, ksel=ksel, nbits=nbits),
        out_shape=[jax.ShapeDtypeStruct((n, LANE), F32), jax.ShapeDtypeStruct((n, LANE), I32)],
        scratch_shapes=[pltpu.VMEM((MAX_COUNT_PREDS, n, LANE), I32)],
        compiler_params=_cparams(),
        name="dsa_decode_select",
    )(keys3, knew2)


def _dsa_decode_kernel(pt_ref, q_ref, kn_ref, vn_ref, t_ref, j_ref, keys_ref, knew_ref, *rest, pg, npages):
    k_refs, v_refs = rest[:pg], rest[pg:2 * pg]
    o_ref, m_ref, l_ref, acc_ref = rest[2 * pg:]
    c = pl.program_id(1)
    qbd_f, blk = _block_diag(q_ref[...], N_HEADS)
    qbd = qbd_f.astype(BF16)
    t, jcut = t_ref[...], j_ref[...]
    lane = lax.broadcasted_iota(I32, (1, LANE), 1)

    def selected(key, col):
        return (key > t) | ((key == t) & (col <= jcut))

    @pl.when(c == 0)
    def _():
        sel_new = selected(knew_ref[...], npages * LANE + lane) & (lane == 0)
        on = jnp.max(jnp.where(sel_new, 1.0, 0.0), axis=1, keepdims=True)
        kn = kn_ref[...].astype(BF16).astype(F32)
        s_new = jnp.sum(qbd_f * kn, axis=1, keepdims=True)
        m_ref[...] = jnp.broadcast_to(jnp.where(on > 0.0, s_new, NEG), m_ref.shape)
        l_ref[...] = jnp.broadcast_to(on, l_ref.shape)
        acc_ref[...] = jnp.broadcast_to(on * vn_ref[...].astype(BF16).astype(F32), acc_ref.shape)

    biases = []
    for i in range(pg):
        col = (c * pg + i) * LANE + lane
        bias = jnp.where(selected(keys_ref[i], col), 0.0, NEG)
        biases.append(jnp.broadcast_to(bias, (N_HEADS, LANE)))
    _pages_update(qbd, k_refs, v_refs, biases, m_ref, l_ref, acc_ref)

    @pl.when(c == pl.num_programs(1) - 1)
    def _():
        _page_finish(o_ref, blk, l_ref, acc_ref)


def _dsa_decode(page_table, q3, kn3, vn3, t3, j3, keys4, knew3, cache_kt, cache_vt, pg):
    n, npages = page_table.shape
    page = cache_kt.shape[2]
    assert page == LANE and npages % pg == 0

    def page_map(i):
        return lambda s, c, pt: (pt[s, c * pg + i], 0, 0)

    seq = lambda s, c, pt: (s, 0, 0)
    in_specs = [pl.BlockSpec((None, 1, HW), seq), pl.BlockSpec((None, 1, HW), seq),
                pl.BlockSpec((None, 1, HW), seq), pl.BlockSpec((None, 1, LANE), seq),
                pl.BlockSpec((None, 1, LANE), seq),
                pl.BlockSpec((pg, None, 1, page), lambda s, c, pt: (c, s, 0, 0)),
                pl.BlockSpec((None, 1, LANE), seq)]
    in_specs += [pl.BlockSpec((None, HW, page), page_map(i)) for i in range(pg)]
    in_specs += [pl.BlockSpec((None, HW, page), page_map(i)) for i in range(pg)]
    return pl.pallas_call(
        functools.partial(_dsa_decode_kernel, pg=pg, npages=npages),
        grid_spec=pltpu.PrefetchScalarGridSpec(
            num_scalar_prefetch=1,
            grid=(n, npages // pg),
            in_specs=in_specs,
            out_specs=pl.BlockSpec((None, 1, HW), seq),
            scratch_shapes=[pltpu.VMEM((N_HEADS, LANE), F32), pltpu.VMEM((N_HEADS, LANE), F32),
                            pltpu.VMEM((N_HEADS, HW), F32)]),
        out_shape=jax.ShapeDtypeStruct((n, 1, HW), F32),
        compiler_params=_cparams("arbitrary", "arbitrary"),
        name="dsa_decode",
    )(page_table, q3, kn3, vn3, t3, j3, keys4, knew3, *([cache_kt] * pg), *([cache_vt] * pg))


def kernel(x_prompt, x_sample, cache_fox_k, cache_fox_v, cache_fox_logf, cache_dsa_k, cache_dsa_v, cache_idx_k, cache_mem_k, cache_mem_v, page_table, mem_prompt, w_in, b_fgate, w_fox_up, w_dsa_up, w_mix_out, ln1_g, ln1_b, w_mq, w_mkv, w_mo, ln2_g, ln2_b, w_ffn_gate, w_ffn_up, w_ffn_down, ln3_g, ln3_b):
    assert w_in.shape[0] == 1, "single-layer step"
    b, s, d = x_prompt.shape
    n, t_new, _ = x_sample.shape
    assert t_new == 1
    npages = page_table.shape[1]
    page = cache_fox_k.shape[2]
    past = npages * page
    ksel_p = min(TOPK_MAX, s // 4)
    ksel_s = min(TOPK_MAX, (past + t_new) // 4)
    mem_tokens = mem_prompt.shape[1]
    mw = MEM_HEADS * HEAD_DIM

    fq, fk, fv, fg, dq, dk, dv, iq, ik, iw, ga, gb = jnp.split(w_in[0], IN_OFFSETS, axis=1)
    misc = jnp.concatenate([iw, fg, jnp.zeros((d, LANE - 2 * N_HEADS), F32)], axis=1)
    wcat = jnp.concatenate([fq * QK_SCALE, fk, fv, dq * QK_SCALE, dk, dv, iq, ga, gb, ik, ik, misc],
                           axis=1).astype(BF16)
    bpad = jnp.zeros((1, LANE), F32).at[0, MISC_LOGF:MISC_LOGF + N_HEADS].set(b_fgate[0].astype(F32))
    wf, wd, wm = w_fox_up[0].astype(BF16), w_dsa_up[0].astype(BF16), w_mix_out[0].astype(BF16)
    wq = (w_mq[0] * QK_SCALE).astype(BF16)
    wkv, wo = w_mkv[0].astype(BF16), w_mo[0].astype(BF16)
    wg, wu, wdn = w_ffn_gate[0].astype(BF16), w_ffn_up[0].astype(BF16), w_ffn_down[0].astype(BF16)
    vec = lambda v: v[0].astype(F32).reshape(1, d)
    g1, b1, g2, b2, g3, b3 = map(vec, (ln1_g, ln1_b, ln2_g, ln2_b, ln3_g, ln3_b))

    tm = min(ROW_TILE, s)
    tq = min(Q_TILE, s)
    tk = min(KV_TILE, s)
    xp = x_prompt.reshape(b * s, d)
    (fq_p, fk32, fk16, fv32, fva, fvb, dq_p, dk32, dk16, dv32, dva, dvb, iq_p, ga_p, gb_p, ik32, ik16,
     misc_p) = _in_proj(xp, wcat, bpad, _rope_tables(jnp.arange(s)), tm, s // tm)
    r3 = lambda a: a.reshape(b, s, a.shape[-1])
    crow, ct = _fox_scan(r3(misc_p), tk)
    o_f = _fox_prompt(r3(fq_p), r3(fk16), r3(fva), r3(fvb), crow, ct, tq)
    o_d = _dsa_prompt(r3(iq_p), r3(ik16), r3(misc_p), r3(dq_p), r3(dk16), r3(dva), r3(dvb), tq, tk, ksel_p)
    h1, qm = _merge(o_f.reshape(b * s, HW), o_d.reshape(b * s, HW), ga_p, gb_p, xp, wf, wd, wm, g1, b1, wq, tm)
    mk32, mv32, mk16, mv16 = _mem_kv(mem_prompt.reshape(b * mem_tokens, d), wkv, min(256, b * mem_tokens))
    om = _mem_prompt(qm.reshape(b, s, mw), mk16.reshape(b, mem_tokens, mw), mv16.reshape(b, mem_tokens, mw), tm)
    y_p = _post(h1, om.reshape(b * s, mw), wo, g2, b2, wg, wu, wdn, g3, b3, tm)

    xs = x_sample.reshape(n, d)
    rows = lambda a: jnp.swapaxes(a, 1, 2).reshape(n, 1, HW)
    pos_s = jnp.full((n,), past, jnp.int32)
    (fq_s, fk32s, _, fv32s, _, _, dq_s, dk32s, _, dv32s, _, _, iq_s, ga_s, gb_s, ik32s, _, misc_s) = _in_proj(
        xs, wcat, bpad, _rope_tables(pos_s), n, 1)
    logf_s = misc_s[:, MISC_LOGF:MISC_LOGF + N_HEADS]
    pg = _pages_per_step(npages, DECODE_PAGES_PER_STEP)
    pg_sc = _pages_per_step(npages, SCORE_PAGES_PER_STEP)
    paged = lambda cch: jnp.transpose(cch[0], (0, 2, 3, 1)).reshape(cch.shape[1], HW, page)
    lft = jnp.swapaxes(cache_fox_logf[0], 1, 2)
    ikt = jnp.swapaxes(cache_idx_k[0], 1, 2)
    o_fs = _fox_decode(page_table, fq_s.reshape(n, 1, HW), rows(fk32s), rows(fv32s),
                       logf_s.reshape(n, N_HEADS, 1), paged(cache_fox_k), paged(cache_fox_v), lft, pg)
    keys4, knew3 = _dsa_scores(page_table, iq_s.reshape(n, N_HEADS, HEAD_DIM),
                               misc_s[:, MISC_IW:MISC_IW + N_HEADS].reshape(n, N_HEADS, 1),
                               ik32s[:, :HEAD_DIM].reshape(n, 1, HEAD_DIM), ikt, pg_sc)
    t_s, j_s = _dsa_select(keys4.reshape(npages, n, page), knew3.reshape(n, LANE), ksel_s)
    o_ds = _dsa_decode(page_table, dq_s.reshape(n, 1, HW), rows(dk32s), rows(dv32s),
                       t_s.reshape(n, 1, LANE), j_s.reshape(n, 1, LANE), keys4, knew3,
                       paged(cache_dsa_k), paged(cache_dsa_v), pg)
    h1s, qms = _merge(o_fs.reshape(n, HW), o_ds.reshape(n, HW), ga_s, gb_s, xs, wf, wd, wm, g1, b1, wq, n)
    memt = lambda cch: jnp.transpose(cch[0], (0, 2, 3, 1)).reshape(n, mw, mem_tokens)
    oms = _mem_sample(qms.reshape(n, 1, mw), memt(cache_mem_k), memt(cache_mem_v))
    y_s = _post(h1s, oms.reshape(n, mw), wo, g2, b2, wg, wu, wdn, g3, b3, n)

    def heads(a, lead):
        nb, _, seq = a.shape
        return jnp.transpose(a.reshape(1, nb, N_HEADS, HEAD_DIM, seq), (0, 1, 4, 2, 3)).reshape(
            1, *lead, N_HEADS, HEAD_DIM)

    return (y_p.reshape(b, s, d), y_s.reshape(n, 1, d),
            heads(fk32, (b, s)), heads(fv32, (b, s)),
            misc_p[:, MISC_LOGF:MISC_LOGF + N_HEADS].reshape(1, b, s, N_HEADS),
            heads(dk32, (b, s)), heads(dv32, (b, s)), ik32[:, :HEAD_DIM].reshape(1, b, s, HEAD_DIM),
            mk32.reshape(1, b, mem_tokens, MEM_HEADS, HEAD_DIM), mv32.reshape(1, b, mem_tokens, MEM_HEADS, HEAD_DIM),
            heads(fk32s, (n, 1)), heads(fv32s, (n, 1)), logf_s.reshape(1, n, 1, N_HEADS),
            heads(dk32s, (n, 1)), heads(dv32s, (n, 1)), ik32s[:, :HEAD_DIM].reshape(1, n, 1, HEAD_DIM))
```

```python
import functools

import numpy as np
import jax
import jax.numpy as jnp
from jax import lax
from jax.experimental import pallas as pl
from jax.experimental.pallas import tpu as pltpu

F32 = jnp.float32
BF16 = jnp.bfloat16
I32 = jnp.int32

LANE = 128
VMEM_LIMIT_BYTES = 56 * 1024 * 1024

HEAD_DIM = 64
N_HEADS = 8
MEM_HEADS = 4
ROT_HALF = 8
ROPE_THETA = 500000.0
LN_EPS = 1e-5
TOPK_MAX = 256
DEEPNORM_ALPHA = 2.0 ** 0.25
QK_SCALE = HEAD_DIM ** -0.5
IDX_SCORE_SCALE = (HEAD_DIM ** -0.5) * (N_HEADS ** -0.5)
FF_CHUNK = 256
ROW_TILE = 256
Q_TILE = 512
KV_TILE = 512
DECODE_PAGES_PER_STEP = 32
MEM_SEQS_PER_STEP = 8
SCORE_PAGES_PER_STEP = 64

NEG = -1e30
INT_MIN = -(2 ** 31)

HW = N_HEADS * HEAD_DIM
IN_WIDTHS = [HW, HW, HW, N_HEADS, HW, HW, HW, HW, HEAD_DIM, N_HEADS, 1024, 1024]
IN_OFFSETS = [int(o) for o in np.cumsum(IN_WIDTHS)[:-1]]

C_FQ, C_FK, C_FV, C_DQ, C_DK, C_DV, C_IQ, C_GA, C_GB, C_IK, C_MISC, C_END = (
    0, 512, 1024, 1536, 2048, 2560, 3072, 3584, 4608, 5632, 5760, 5888)
MISC_IW = 0
MISC_LOGF = 8


def _cparams(*sem):
    return pltpu.CompilerParams(dimension_semantics=sem or None, vmem_limit_bytes=VMEM_LIMIT_BYTES)


def _dot(a, b):
    return jnp.dot(a, b, preferred_element_type=F32)


def _dot_nt(a, b):
    return lax.dot_general(a, b, (((1,), (1,)), ((), ())), preferred_element_type=F32)


def _sigmoid(x):
    return 1.0 / (1.0 + jnp.exp(-x))


def _layer_norm(x, g, b):
    mu = jnp.mean(x, axis=-1, keepdims=True)
    xc = x - mu
    var = jnp.mean(xc * xc, axis=-1, keepdims=True)
    return xc * lax.rsqrt(var + LN_EPS) * g + b


def _pages_per_step(npages, want):
    return max(p for p in range(1, min(npages, want) + 1) if npages % p == 0)


def _half_mask(q2, odd):
    lane = lax.broadcasted_iota(I32, q2.shape, 1)
    keep = (lane >= HEAD_DIM) if odd else (lane < HEAD_DIM)
    return jnp.where(keep, q2.astype(F32), 0.0).astype(BF16)


def _inproj_kernel(x_ref, w_ref, cos_ref, sup_ref, sdn_ref, b_ref,
                   fq_ref, fk32_ref, fk16_ref, fv32_ref, fva_ref, fvb_ref,
                   dq_ref, dk32_ref, dk16_ref, dv32_ref, dva_ref, dvb_ref,
                   iq_ref, ga_ref, gb_ref, ik32_ref, ik16_ref, misc_ref):
    xb = x_ref[...].astype(BF16)
    cosv, sup, sdn = cos_ref[...], sup_ref[...], sdn_ref[...]

    def proj(off, width):
        return _dot(xb, w_ref[:, off:off + width])

    def rope(z):
        return (z * cosv + pltpu.roll(z, LANE - ROT_HALF, 1) * sup
                + pltpu.roll(z, ROT_HALF, 1) * sdn)

    def ones_halves(z):
        even = (lax.broadcasted_iota(I32, z.shape, 1) & HEAD_DIM) == 0
        return jnp.where(even, z, 1.0).astype(BF16), jnp.where(even, 1.0, z).astype(BF16)

    fq_ref[...] = proj(C_FQ, HW).astype(BF16)
    z = proj(C_FK, HW)
    fk32_ref[...] = z.T
    fk16_ref[...] = z.astype(BF16)
    z = proj(C_FV, HW)
    fv32_ref[...] = z.T
    fva_ref[...], fvb_ref[...] = ones_halves(z)
    z = proj(C_DQ, HW)
    for g in range(HW // LANE):
        dq_ref[:, g * LANE:(g + 1) * LANE] = rope(z[:, g * LANE:(g + 1) * LANE]).astype(BF16)
    z = proj(C_DK, HW)
    for g in range(HW // LANE):
        r = rope(z[:, g * LANE:(g + 1) * LANE])
        dk32_ref[g * LANE:(g + 1) * LANE, :] = r.T
        dk16_ref[:, g * LANE:(g + 1) * LANE] = r.astype(BF16)
    z = proj(C_DV, HW)
    dv32_ref[...] = z.T
    dva_ref[...], dvb_ref[...] = ones_halves(z)
    z = proj(C_IQ, HW)
    for g in range(HW // LANE):
        iq_ref[:, g * LANE:(g + 1) * LANE] = rope(z[:, g * LANE:(g + 1) * LANE]).astype(BF16)
    ga_ref[...] = proj(C_GA, 1024)
    gb_ref[...] = proj(C_GB, 1024)
    r = rope(proj(C_IK, LANE))
    ik32_ref[...] = r
    ik16_ref[...] = r.astype(BF16)
    z = proj(C_MISC, LANE)
    lane = lax.broadcasted_iota(I32, z.shape, 1)
    t = z + b_ref[...]
    logf = -(jnp.maximum(-t, 0.0) + jnp.log1p(jnp.exp(-jnp.abs(t))))
    misc_ref[...] = jnp.where(lane < MISC_LOGF, z * IDX_SCORE_SCALE,
                              jnp.where(lane < MISC_LOGF + N_HEADS, logf, 0.0))


def _in_proj(x2, wcat, bpad, tables, tm, table_blocks):
    n, d = x2.shape
    seq = table_blocks * tm
    row = lambda i: (i, 0)
    tab = lambda i: (i % table_blocks, 0)
    const = lambda i: (0, 0)
    feature_major = lambda i: (i // table_blocks, 0, i % table_blocks)
    outs = [(HW, BF16), (HW, None), (HW, BF16), (HW, None), (HW, BF16), (HW, BF16),
            (HW, BF16), (HW, None), (HW, BF16), (HW, None), (HW, BF16), (HW, BF16),
            (HW, BF16), (1024, F32), (1024, F32), (LANE, F32), (LANE, BF16), (LANE, F32)]
    return pl.pallas_call(
        _inproj_kernel,
        grid=(n // tm,),
        in_specs=[pl.BlockSpec((tm, d), row),
                  pl.BlockSpec((d, C_END), const, pipeline_mode=pl.Buffered(1)),
                  pl.BlockSpec((tm, LANE), tab), pl.BlockSpec((tm, LANE), tab),
                  pl.BlockSpec((tm, LANE), tab), pl.BlockSpec((1, LANE), const)],
        out_specs=[pl.BlockSpec((None, w, tm), feature_major) if dt is None else pl.BlockSpec((tm, w), row)
                   for w, dt in outs],
        out_shape=[jax.ShapeDtypeStruct((n // seq, w, seq), F32) if dt is None
                   else jax.ShapeDtypeStruct((n, w), dt) for w, dt in outs],
        compiler_params=_cparams("arbitrary"),
        name="in_proj",
    )(x2, wcat, *tables, bpad)


def _rope_tables(pos):
    p = pos.shape[0]
    inv_freq = ROPE_THETA ** (-jnp.arange(ROT_HALF, dtype=F32) / ROT_HALF)
    ang = pos.astype(F32)[:, None] * inv_freq[None, :]
    cos, sin = jnp.cos(ang), jnp.sin(ang)
    zeros = lambda w: jnp.zeros((p, w), F32)
    cos64 = jnp.concatenate([cos, cos, jnp.ones((p, HEAD_DIM - 2 * ROT_HALF), F32)], axis=1)
    sup64 = jnp.concatenate([-sin, zeros(HEAD_DIM - ROT_HALF)], axis=1)
    sdn64 = jnp.concatenate([zeros(ROT_HALF), sin, zeros(HEAD_DIM - 2 * ROT_HALF)], axis=1)
    two = lambda t: jnp.concatenate([t, t], axis=1)
    return two(cos64), two(sup64), two(sdn64)


def _scan_kernel(m_ref, crow_ref, ct_ref):
    x = m_ref[...]
    s = x.shape[0]
    row = lax.broadcasted_iota(I32, x.shape, 0)
    k = 1
    while k < s:
        x = x + jnp.where(row >= k, pltpu.roll(x, k, 0), 0.0)
        k *= 2
    crow_ref[...] = x
    xt = x.T
    nblk, _, tk = ct_ref.shape
    for j in range(nblk):
        ct_ref[j] = xt[MISC_LOGF:MISC_LOGF + N_HEADS, j * tk:(j + 1) * tk]


def _fox_scan(misc3, tk):
    b, s, _ = misc3.shape
    return pl.pallas_call(
        _scan_kernel,
        grid=(b,),
        in_specs=[pl.BlockSpec((None, s, LANE), lambda i: (i, 0, 0))],
        out_specs=[pl.BlockSpec((None, s, LANE), lambda i: (i, 0, 0)),
                   pl.BlockSpec((None, s // tk, N_HEADS, tk), lambda i: (i, 0, 0, 0))],
        out_shape=[jax.ShapeDtypeStruct((b, s, LANE), F32),
                   jax.ShapeDtypeStruct((b, s // tk, N_HEADS, tk), F32)],
        compiler_params=_cparams("arbitrary"),
        name="fox_scan",
    )(misc3)


def _flash_scratch(tq):
    return [pltpu.VMEM((N_HEADS, tq, LANE), BF16),
            pltpu.VMEM((N_HEADS, tq, LANE), F32),
            pltpu.VMEM((N_HEADS // 2, tq, LANE), F32),
            pltpu.VMEM((N_HEADS // 2, tq, LANE), F32)]


def _flash_start(q_ref, qh_ref, m_ref, l_ref, acc_ref):
    for h in range(N_HEADS):
        g = h // 2
        qh_ref[h] = _half_mask(q_ref[:, g * LANE:(g + 1) * LANE], h % 2)
    m_ref[...] = jnp.full(m_ref.shape, NEG, F32)
    l_ref[...] = jnp.zeros(l_ref.shape, F32)
    acc_ref[...] = jnp.zeros(acc_ref.shape, F32)


def _flash_block(qh_ref, m_ref, l_ref, acc_ref, k_ref, va_ref, vb_ref, r0, tk, bias_fn, mask=None):
    tq = qh_ref.shape[1]
    even = lax.broadcasted_iota(I32, (tq, LANE), 1) < HEAD_DIM
    for g in range(N_HEADS // 2):
        gs = slice(g * LANE, (g + 1) * LANE)
        kb = k_ref[pl.ds(r0, tk), gs]
        vs = (va_ref[pl.ds(r0, tk), gs], vb_ref[pl.ds(r0, tk), gs])
        alphas, pvs = [], []
        for e in range(2):
            h = 2 * g + e
            s = _dot_nt(qh_ref[h], kb)
            s = [s[:, c * LANE:(c + 1) * LANE] + bias_fn(h, c) for c in range(tk // LANE)]
            if mask is not None:
                s = [jnp.where(mask[:, c * LANE:(c + 1) * LANE], sc, NEG) for c, sc in enumerate(s)]
            m_old = m_ref[h]
            m_new = jnp.maximum(m_old, jnp.max(functools.reduce(jnp.maximum, s), axis=1, keepdims=True))
            p = jnp.concatenate([jnp.exp(sc - m_new) for sc in s], axis=1)
            m_ref[h] = m_new
            alphas.append(jnp.exp(m_old - m_new))
            pvs.append(_dot(p.astype(BF16), vs[e]))
        acc_ref[g] = jnp.where(even, alphas[0], alphas[1]) * acc_ref[g] + jnp.where(even, pvs[0], pvs[1])
        l_ref[g] = jnp.where(even, alphas[1], alphas[0]) * l_ref[g] + jnp.where(even, pvs[1], pvs[0])


def _flash_finish(o_ref, l_ref, acc_ref):
    for g in range(N_HEADS // 2):
        l = pltpu.roll(l_ref[g], HEAD_DIM, 1)
        o_ref[:, g * LANE:(g + 1) * LANE] = (acc_ref[g] / l).astype(BF16)


def _fox_prompt_kernel(q_ref, k_ref, va_ref, vb_ref, crow_ref, ct_ref, o_ref,
                       qh_ref, m_ref, l_ref, acc_ref, cq_ref):
    qi = pl.program_id(1)
    tq = q_ref.shape[0]
    tk = ct_ref.shape[2]
    _flash_start(q_ref, qh_ref, m_ref, l_ref, acc_ref)
    for h in range(N_HEADS):
        cq_ref[h] = jnp.broadcast_to(crow_ref[:, MISC_LOGF + h:MISC_LOGF + h + 1], (tq, LANE))

    def block(j, mask):
        r0 = pl.multiple_of(j * tk, tk)
        _flash_block(qh_ref, m_ref, l_ref, acc_ref, k_ref, va_ref, vb_ref, r0, tk,
                     lambda h, c: cq_ref[h] - ct_ref[j, h:h + 1, c * LANE:(c + 1) * LANE], mask)

    def body(j, _):
        block(j, None)
        return 0

    nfull = (qi * tq) // tk
    lax.fori_loop(0, nfull, body, 0)
    causal = (nfull * tk + lax.broadcasted_iota(I32, (tq, tk), 1)
              <= qi * tq + lax.broadcasted_iota(I32, (tq, tk), 0))
    block(nfull, causal)
    _flash_finish(o_ref, l_ref, acc_ref)


def _fox_prompt(q, k, va, vb, crow, ct, tq):
    b, s, _ = q.shape
    nblk, tk = ct.shape[1], ct.shape[3]
    assert tk % tq == 0
    qmap = lambda i, j: (i, j, 0)
    full = lambda i, j: (i, 0, 0)
    return pl.pallas_call(
        _fox_prompt_kernel,
        grid=(b, s // tq),
        in_specs=[pl.BlockSpec((None, tq, HW), qmap),
                  pl.BlockSpec((None, s, HW), full), pl.BlockSpec((None, s, HW), full),
                  pl.BlockSpec((None, s, HW), full),
                  pl.BlockSpec((None, tq, LANE), qmap),
                  pl.BlockSpec((None, nblk, N_HEADS, tk), lambda i, j: (i, 0, 0, 0))],
        out_specs=pl.BlockSpec((None, tq, HW), qmap),
        out_shape=jax.ShapeDtypeStruct((b, s, HW), BF16),
        scratch_shapes=_flash_scratch(tq) + [pltpu.VMEM((N_HEADS, tq, LANE), F32)],
        compiler_params=_cparams("arbitrary", "arbitrary"),
        name="fox_prompt",
    )(q, k, va, vb, crow, ct)


COUNT_ROWS = 128
MAX_COUNT_PREDS = 2


def _key_to_float(key):
    return lax.bitcast_convert_type(jnp.where(key < 0, key ^ 0x7FFFFFFF, key), F32)


def _kth_largest(count, ksel, shape):
    ge = lambda cand: (lambda sc, col, rs, f=_key_to_float(cand): sc >= f[rs])
    (c,) = count([lambda sc, col, rs: sc >= 0.0])
    t0 = jnp.where(c >= ksel, jnp.zeros(shape, I32), jnp.full(shape, INT_MIN, I32))

    def body(i, t):
        cand = t | lax.shift_left(jnp.int32(1), 30 - i)
        (c,) = count([ge(cand)])
        return jnp.where(c >= ksel, cand, t)

    t = lax.fori_loop(0, 31, body, t0)
    return jnp.where(t == INT_MIN, -jnp.inf, _key_to_float(t))


def _tie_cut(count, t, ksel, nbits, shape):
    n_gt, n_ge = count([lambda sc, col, rs: sc > t[rs], lambda sc, col, rs: sc >= t[rs]])
    need = ksel - n_gt
    surplus = (n_ge > ksel) & (t > -jnp.inf)
    keep_all = jnp.full(shape, (1 << nbits) - 1, I32)

    def search():
        def body(i, j):
            cand = j | lax.shift_left(jnp.int32(1), nbits - 1 - i)
            (c,) = count([lambda sc, col, rs: (sc == t[rs]) & (col < cand[rs])])
            return jnp.where(c < need, cand, j)
        return lax.fori_loop(0, nbits, body, jnp.zeros(shape, I32))

    return lax.cond(jnp.max(surplus.astype(I32)) > 0, search, lambda: keep_all)


def _dsa_prompt_kernel(iq_ref, ik_ref, misc_ref, q_ref, k_ref, va_ref, vb_ref, o_ref,
                       sc_ref, lhs_ref, qh_ref, m_ref, l_ref, acc_ref, *, ksel, nbits):
    qi = pl.program_id(1)
    tq = q_ref.shape[0]
    kc = sc_ref.shape[2]
    nch = (qi * tq) // kc + 1
    assert kc == tq
    sub = 8

    for h in range(N_HEADS):
        g = h // 2
        lhs_ref[h] = _half_mask(iq_ref[:, g * LANE:(g + 1) * LANE], h % 2)
    wt = misc_ref[...].T
    wrows = [wt[MISC_IW + h:MISC_IW + h + 1, :] for h in range(N_HEADS)]
    key_l = lax.broadcasted_iota(I32, (kc, tq), 0)
    qry_g = qi * tq + lax.broadcasted_iota(I32, (kc, tq), 1)

    def score_chunk(j, _):
        ikc = ik_ref[pl.ds(pl.multiple_of(j * kc, kc), kc), :]
        sc = jnp.zeros((kc, tq), F32)
        for h in range(N_HEADS):
            sc = sc + jnp.maximum(_dot_nt(ikc, lhs_ref[h]), 0.0) * wrows[h]
        sc_ref[j] = jnp.where(j * kc + key_l <= qry_g, sc, -jnp.inf)
        return 0

    lax.fori_loop(0, nch, score_chunk, 0)

    slab_l = lax.broadcasted_iota(I32, (sub, tq), 0)
    every = slice(None)

    def count(preds):
        def body(j, parts):
            parts = list(parts)
            for r in range(kc // sub):
                tile = sc_ref[j, r * sub:(r + 1) * sub, :]
                col = j * kc + r * sub + slab_l
                for i, pred in enumerate(preds):
                    parts[i] = parts[i] + pred(tile, col, every).astype(I32)
            return tuple(parts)

        parts = lax.fori_loop(0, nch, body, tuple(jnp.zeros((sub, tq), I32) for _ in preds))
        return [jnp.sum(part, axis=0, keepdims=True) for part in parts]

    t = _kth_largest(count, ksel, (1, tq))
    jcut = _tie_cut(count, t, ksel, nbits, (1, tq))

    def bias_chunk(j, _):
        col = j * kc + key_l
        tile = sc_ref[j]
        sel = ((tile > t) | ((tile == t) & (col <= jcut))) & (col <= qry_g)
        sc_ref[j] = jnp.where(sel, 0.0, NEG).T
        return 0

    lax.fori_loop(0, nch, bias_chunk, 0)

    _flash_start(q_ref, qh_ref, m_ref, l_ref, acc_ref)

    def block(j, _):
        _flash_block(qh_ref, m_ref, l_ref, acc_ref, k_ref, va_ref, vb_ref, pl.multiple_of(j * kc, kc), kc,
                     lambda h, c: sc_ref[j, :, c * LANE:(c + 1) * LANE])
        return 0

    lax.fori_loop(0, nch, block, 0)
    _flash_finish(o_ref, l_ref, acc_ref)


def _dsa_prompt(iq, ik2, misc3, q, k, va, vb, tq, tk, ksel):
    b, s, _ = q.shape
    assert tk == tq
    nbits = max(1, int(s - 1).bit_length())
    qmap = lambda i, j: (i, j, 0)
    full = lambda i, j: (i, 0, 0)
    return pl.pallas_call(
        functools.partial(_dsa_prompt_kernel, ksel=ksel, nbits=nbits),
        grid=(b, s // tq),
        in_specs=[pl.BlockSpec((None, tq, HW), qmap),
                  pl.BlockSpec((None, s, LANE), full),
                  pl.BlockSpec((None, tq, LANE), qmap),
                  pl.BlockSpec((None, tq, HW), qmap),
                  pl.BlockSpec((None, s, HW), full),
                  pl.BlockSpec((None, s, HW), full),
                  pl.BlockSpec((None, s, HW), full)],
        out_specs=pl.BlockSpec((None, tq, HW), qmap),
        out_shape=jax.ShapeDtypeStruct((b, s, HW), BF16),
        scratch_shapes=[pltpu.VMEM((s // tk, tq, tk), F32),
                        pltpu.VMEM((N_HEADS, tq, LANE), BF16)
                        ] + _flash_scratch(tq),
        compiler_params=_cparams("arbitrary", "arbitrary"),
        name="dsa_prompt",
    )(iq, ik2, misc3, q, k, va, vb)


def _merge_kernel(of_ref, od_ref, ga_ref, gb_ref, x_ref, wf_ref, wd_ref, wm_ref, g_ref, b_ref, wq_ref,
                  h_ref, qm_ref):
    a1 = _dot(of_ref[...].astype(BF16), wf_ref[...])
    a2 = _dot(od_ref[...].astype(BF16), wd_ref[...])
    u = _sigmoid(ga_ref[...]) * a1 + _sigmoid(gb_ref[...]) * a2
    a = _dot(u.astype(BF16), wm_ref[...])
    h = _layer_norm(DEEPNORM_ALPHA * x_ref[...] + a, g_ref[...], b_ref[...])
    h_ref[...] = h
    qm_ref[...] = _dot(h.astype(BF16), wq_ref[...]).astype(BF16)


def _merge(of, od, ga, gb, x2, wf, wd, wm, g1, b1, wq, tm):
    n, d = x2.shape
    mw = wq.shape[1]
    row = lambda i: (i, 0)
    const = lambda i: (0, 0)
    wspec = lambda w: pl.BlockSpec(w.shape, const, pipeline_mode=pl.Buffered(1))
    return pl.pallas_call(
        _merge_kernel,
        grid=(n // tm,),
        in_specs=[pl.BlockSpec((tm, HW), row), pl.BlockSpec((tm, HW), row),
                  pl.BlockSpec((tm, d), row), pl.BlockSpec((tm, d), row), pl.BlockSpec((tm, d), row),
                  wspec(wf), wspec(wd), wspec(wm), wspec(g1), wspec(b1), wspec(wq)],
        out_specs=[pl.BlockSpec((tm, d), row), pl.BlockSpec((tm, mw), row)],
        out_shape=[jax.ShapeDtypeStruct((n, d), F32), jax.ShapeDtypeStruct((n, mw), BF16)],
        compiler_params=_cparams("arbitrary"),
        name="merge_ln1",
    )(of, od, ga, gb, x2, wf, wd, wm, g1, b1, wq)


def _memkv_kernel(x_ref, w_ref, k32_ref, v32_ref, k16_ref, v16_ref):
    z = _dot(x_ref[...].astype(BF16), w_ref[...])
    mw = k32_ref.shape[1]
    k32_ref[...] = z[:, :mw]
    v32_ref[...] = z[:, mw:]
    k16_ref[...] = z[:, :mw].astype(BF16)
    v16_ref[...] = z[:, mw:].astype(BF16)


def _mem_kv(mem2, wkv, tm):
    n, d = mem2.shape
    mw = wkv.shape[1] // 2
    row = lambda i: (i, 0)
    return pl.pallas_call(
        _memkv_kernel,
        grid=(n // tm,),
        in_specs=[pl.BlockSpec((tm, d), row), pl.BlockSpec(wkv.shape, lambda i: (0, 0))],
        out_specs=[pl.BlockSpec((tm, mw), row)] * 4,
        out_shape=[jax.ShapeDtypeStruct((n, mw), F32), jax.ShapeDtypeStruct((n, mw), F32),
                   jax.ShapeDtypeStruct((n, mw), BF16), jax.ShapeDtypeStruct((n, mw), BF16)],
        compiler_params=_cparams("arbitrary"),
        name="mem_kv",
    )(mem2, wkv)


def _mem_prompt_kernel(q_ref, k_ref, v_ref, o_ref):
    tm = q_ref.shape[0]
    lane = lax.broadcasted_iota(I32, (tm, LANE), 1)
    for g in range(q_ref.shape[1] // LANE):
        gs = slice(g * LANE, (g + 1) * LANE)
        q2, kb, vb = q_ref[:, gs], k_ref[:, gs], v_ref[:, gs]
        outs = []
        for e in range(2):
            s = _dot_nt(_half_mask(q2, e), kb)
            p = jnp.exp(s - jnp.max(s, axis=1, keepdims=True))
            outs.append(_dot(p.astype(BF16), vb) / jnp.sum(p, axis=1, keepdims=True))
        o_ref[:, gs] = jnp.where(lane < HEAD_DIM, outs[0], outs[1]).astype(BF16)


def _mem_prompt(qm, mk, mv, tm):
    b, s, mw = qm.shape
    m = mk.shape[1]
    return pl.pallas_call(
        _mem_prompt_kernel,
        grid=(b, s // tm),
        in_specs=[pl.BlockSpec((None, tm, mw), lambda i, j: (i, j, 0)),
                  pl.BlockSpec((None, m, mw), lambda i, j: (i, 0, 0)),
                  pl.BlockSpec((None, m, mw), lambda i, j: (i, 0, 0))],
        out_specs=pl.BlockSpec((None, tm, mw), lambda i, j: (i, j, 0)),
        out_shape=jax.ShapeDtypeStruct((b, s, mw), BF16),
        compiler_params=_cparams("arbitrary", "arbitrary"),
        name="mem_attn_prompt",
    )(qm, mk, mv)


def _block_diag(row_vec, nrows):
    w = row_vec.shape[1]
    rows = lax.broadcasted_iota(I32, (nrows, w), 0)
    lanes = lax.broadcasted_iota(I32, (nrows, w), 1)
    blk = rows == lax.shift_right_logical(lanes, 6)
    return jnp.where(blk, jnp.broadcast_to(row_vec.astype(F32), (nrows, w)), 0.0), blk


def _mem_sample_kernel(q_ref, kt_ref, vt_ref, o_ref):
    for i in range(q_ref.shape[0]):
        qbd, blk = _block_diag(q_ref[i], 8)
        s = _dot(qbd.astype(BF16), kt_ref[i].astype(BF16))
        p = jnp.exp(s - jnp.max(s, axis=1, keepdims=True))
        o = _dot_nt(p.astype(BF16), vt_ref[i].astype(BF16)) / jnp.sum(p, axis=1, keepdims=True)
        o_ref[i] = jnp.sum(jnp.where(blk, o, 0.0), axis=0, keepdims=True)


def _mem_sample(qm3, ck, cv):
    n, mw, m = ck.shape
    g = _pages_per_step(n, MEM_SEQS_PER_STEP)
    return pl.pallas_call(
        _mem_sample_kernel,
        grid=(n // g,),
        in_specs=[pl.BlockSpec((g, 1, mw), lambda i: (i, 0, 0)),
                  pl.BlockSpec((g, mw, m), lambda i: (i, 0, 0)),
                  pl.BlockSpec((g, mw, m), lambda i: (i, 0, 0))],
        out_specs=pl.BlockSpec((g, 1, mw), lambda i: (i, 0, 0)),
        out_shape=jax.ShapeDtypeStruct((n, 1, mw), F32),
        compiler_params=_cparams("arbitrary"),
        name="mem_attn_sample",
    )(qm3, ck, cv)


def _post_kernel(h_ref, om_ref, wo_ref, g2_ref, b2_ref, wg_ref, wu_ref, wdn_ref, g3_ref, b3_ref, y_ref):
    h2 = _layer_norm(DEEPNORM_ALPHA * h_ref[...] + _dot(om_ref[...].astype(BF16), wo_ref[...]),
                     g2_ref[...], b2_ref[...])
    hb = h2.astype(BF16)
    ff = wg_ref.shape[1]
    acc = jnp.zeros(h2.shape, F32)
    for c in range(ff // FF_CHUNK):
        cs = slice(c * FF_CHUNK, (c + 1) * FF_CHUNK)
        gate = _dot(hb, wg_ref[:, cs])
        act = gate * _sigmoid(gate) * _dot(hb, wu_ref[:, cs])
        acc = acc + _dot(act.astype(BF16), wdn_ref[cs, :])
    y_ref[...] = _layer_norm(DEEPNORM_ALPHA * h2 + acc, g3_ref[...], b3_ref[...])


def _post(h1, om, wo, g2, b2, wg, wu, wdn, g3, b3, tm):
    n, d = h1.shape
    row = lambda i: (i, 0)
    const = lambda i: (0, 0)
    wspec = lambda w: pl.BlockSpec(w.shape, const, pipeline_mode=pl.Buffered(1))
    return pl.pallas_call(
        _post_kernel,
        grid=(n // tm,),
        in_specs=[pl.BlockSpec((tm, d), row), pl.BlockSpec((tm, om.shape[1]), row),
                  wspec(wo), wspec(g2), wspec(b2), wspec(wg), wspec(wu), wspec(wdn), wspec(g3), wspec(b3)],
        out_specs=pl.BlockSpec((tm, d), row),
        out_shape=jax.ShapeDtypeStruct((n, d), F32),
        compiler_params=_cparams("arbitrary"),
        name="post_ffn",
    )(h1, om, wo, g2, b2, wg, wu, wdn, g3, b3)


def _pages_update(qbd, kt_refs, vt_refs, biases, m_ref, l_ref, acc_ref):
    s = [_dot(qbd, kt[...].astype(BF16)) + b for kt, b in zip(kt_refs, biases)]
    smax = functools.reduce(jnp.maximum, s)
    m = m_ref[...]
    m_new = jnp.maximum(m, jnp.max(smax, axis=1, keepdims=True))
    alpha = jnp.exp(m - m_new)
    p = [jnp.exp(si - m_new) for si in s]
    l_ref[...] = alpha * l_ref[...] + jnp.sum(functools.reduce(jnp.add, p), axis=1, keepdims=True)
    pv = functools.reduce(jnp.add, [_dot_nt(pi.astype(BF16), vt[...].astype(BF16)) for pi, vt in zip(p, vt_refs)])
    acc_ref[...] = alpha[:, :1] * acc_ref[...] + pv
    m_ref[...] = m_new


def _page_finish(o_ref, blk, l_ref, acc_ref):
    o = acc_ref[...] / l_ref[:, :1]
    o_ref[...] = jnp.sum(jnp.where(blk, o, 0.0), axis=0, keepdims=True)


def _fox_decode_kernel(pt_ref, q_ref, kn_ref, vn_ref, lfn_ref, *rest, pg):
    k_refs, v_refs, lf_refs = rest[:pg], rest[pg:2 * pg], rest[2 * pg:3 * pg]
    o_ref, m_ref, l_ref, acc_ref, suf_ref = rest[3 * pg:]
    c = pl.program_id(1)
    qbd_f, blk = _block_diag(q_ref[...], N_HEADS)
    qbd = qbd_f.astype(BF16)

    @pl.when(c == 0)
    def _():
        kn = kn_ref[...].astype(BF16).astype(F32)
        m_ref[...] = jnp.broadcast_to(jnp.sum(qbd_f * kn, axis=1, keepdims=True), m_ref.shape)
        l_ref[...] = jnp.ones(l_ref.shape, F32)
        acc_ref[...] = jnp.broadcast_to(vn_ref[...].astype(BF16).astype(F32), acc_ref.shape)
        suf_ref[...] = jnp.broadcast_to(lfn_ref[...], suf_ref.shape)

    lane = lax.broadcasted_iota(I32, (N_HEADS, LANE), 1)
    suf = suf_ref[...]
    biases = []
    for i in range(pg):
        y = lf_refs[i][...]
        k = 1
        while k < LANE:
            y = y + jnp.where(lane + k < LANE, pltpu.roll(y, LANE - k, 1), 0.0)
            k *= 2
        biases.append(suf + jnp.where(lane < LANE - 1, pltpu.roll(y, LANE - 1, 1), 0.0))
        suf = suf + y[:, :1]
    suf_ref[...] = suf
    _pages_update(qbd, k_refs, v_refs, biases, m_ref, l_ref, acc_ref)

    @pl.when(c == pl.num_programs(1) - 1)
    def _():
        _page_finish(o_ref, blk, l_ref, acc_ref)


def _fox_decode(page_table, q3, kn3, vn3, lfn3, cache_kt, cache_vt, cache_lft, pg):
    n, npages = page_table.shape
    page = cache_kt.shape[2]
    assert page == LANE and npages % pg == 0

    def page_map(i):
        return lambda s, c, pt: (pt[s, npages - 1 - (c * pg + i)], 0, 0)

    seq = lambda s, c, pt: (s, 0, 0)
    in_specs = [pl.BlockSpec((None, 1, HW), seq), pl.BlockSpec((None, 1, HW), seq),
                pl.BlockSpec((None, 1, HW), seq), pl.BlockSpec((None, N_HEADS, 1), seq)]
    in_specs += [pl.BlockSpec((None, HW, page), page_map(i)) for i in range(pg)]
    in_specs += [pl.BlockSpec((None, HW, page), page_map(i)) for i in range(pg)]
    in_specs += [pl.BlockSpec((None, N_HEADS, page), page_map(i)) for i in range(pg)]
    return pl.pallas_call(
        functools.partial(_fox_decode_kernel, pg=pg),
        grid_spec=pltpu.PrefetchScalarGridSpec(
            num_scalar_prefetch=1,
            grid=(n, npages // pg),
            in_specs=in_specs,
            out_specs=pl.BlockSpec((None, 1, HW), seq),
            scratch_shapes=[pltpu.VMEM((N_HEADS, LANE), F32), pltpu.VMEM((N_HEADS, LANE), F32),
                            pltpu.VMEM((N_HEADS, HW), F32), pltpu.VMEM((N_HEADS, LANE), F32)]),
        out_shape=jax.ShapeDtypeStruct((n, 1, HW), F32),
        compiler_params=_cparams("arbitrary", "arbitrary"),
        name="fox_decode",
    )(page_table, q3, kn3, vn3, lfn3, *([cache_kt] * pg), *([cache_vt] * pg), *([cache_lft] * pg))


def _dsa_score_kernel(pt_ref, iq_ref, w_ref, ikn_ref, *rest, pg):
    pages = rest[:pg]
    keys_ref, knew_ref = rest[pg], rest[pg + 1]
    iq = iq_ref[...]
    w = w_ref[...]
    for i in range(pg):
        r = _dot(iq, pages[i][...].astype(BF16))
        keys_ref[i] = jnp.sum(jnp.maximum(r, 0.0) * w, axis=0, keepdims=True)

    @pl.when(pl.program_id(1) == 0)
    def _():
        ikn = ikn_ref[...].astype(BF16).astype(F32)
        r = jnp.sum(iq.astype(F32) * ikn, axis=1, keepdims=True)
        sc = jnp.sum(jnp.maximum(r, 0.0) * w, axis=0, keepdims=True)
        lane = lax.broadcasted_iota(I32, knew_ref.shape, 1)
        knew_ref[...] = jnp.where(lane == 0, jnp.broadcast_to(sc, knew_ref.shape), -jnp.inf)


def _dsa_scores(page_table, iq3, w3, ikn3, cache_ikt, pg):
    n, npages = page_table.shape
    page = cache_ikt.shape[2]
    assert page == LANE and npages % pg == 0

    def page_map(i):
        return lambda s, c, pt: (pt[s, c * pg + i], 0, 0)

    seq = lambda s, c, pt: (s, 0, 0)
    in_specs = [pl.BlockSpec((None, N_HEADS, HEAD_DIM), seq), pl.BlockSpec((None, N_HEADS, 1), seq),
                pl.BlockSpec((None, 1, HEAD_DIM), seq)]
    in_specs += [pl.BlockSpec((None, HEAD_DIM, page), page_map(i)) for i in range(pg)]
    return pl.pallas_call(
        functools.partial(_dsa_score_kernel, pg=pg),
        grid_spec=pltpu.PrefetchScalarGridSpec(
            num_scalar_prefetch=1,
            grid=(n, npages // pg),
            in_specs=in_specs,
            out_specs=[pl.BlockSpec((pg, None, 1, page), lambda s, c, pt: (c, s, 0, 0)),
                       pl.BlockSpec((None, 1, LANE), seq)]),
        out_shape=[jax.ShapeDtypeStruct((npages, n, 1, page), F32),
                   jax.ShapeDtypeStruct((n, 1, LANE), F32)],
        compiler_params=_cparams("arbitrary", "arbitrary"),
        name="dsa_decode_scores",
    )(page_table, iq3, w3, ikn3, *([cache_ikt] * pg))


def _select_kernel(keys_ref, knew_ref, t_ref, j_ref, cnt_ref, *, ksel, nbits):
    ng, n, _ = keys_ref.shape
    lane = lax.broadcasted_iota(I32, (n, LANE), 1)

    rb = min(COUNT_ROWS, n)
    lane_rb = lax.broadcasted_iota(I32, (rb, LANE), 1)

    def count(preds):
        for r in range(n // rb):
            rs = slice(r * rb, (r + 1) * rb)

            def body(g, parts, rs=rs):
                tile, col = keys_ref[g, rs, :], g * LANE + lane_rb
                return tuple(part + pred(tile, col, rs).astype(I32) for part, pred in zip(parts, preds))

            parts = lax.fori_loop(0, ng, body, tuple(jnp.zeros((rb, LANE), I32) for _ in preds))
            for i, (part, pred) in enumerate(zip(parts, preds)):
                part = part + pred(knew_ref[rs, :], ng * LANE + lane_rb, rs).astype(I32)
                cnt_ref[i, rs, :] = part
        return [jnp.broadcast_to(jnp.sum(cnt_ref[i], axis=1, keepdims=True), cnt_ref.shape[1:])
                for i in range(len(preds))]

    t = _kth_largest(count, ksel, (n, LANE))
    t_ref[...] = t
    j_ref[...] = _tie_cut(count, t, ksel, nbits, (n, LANE))


def _dsa_select(keys3, knew2, ksel):
    ng, n, _ = keys3.shape
    nbits = max(1, int((ng + 1) * LANE - 1).bit_length())
    return pl.pallas_call(
        functools.partial(_select_kernel, ksel=ksel, nbits=nbits),
        out_shape=[jax.ShapeDtypeStruct((n, LANE), F32), jax.ShapeDtypeStruct((n, LANE), I32)],
        scratch_shapes=[pltpu.VMEM((MAX_COUNT_PREDS, n, LANE), I32)],
        compiler_params=_cparams(),
        name="dsa_decode_select",
    )(keys3, knew2)


def _dsa_decode_kernel(pt_ref, q_ref, kn_ref, vn_ref, t_ref, j_ref, keys_ref, knew_ref, *rest, pg, npages):
    k_refs, v_refs = rest[:pg], rest[pg:2 * pg]
    o_ref, m_ref, l_ref, acc_ref = rest[2 * pg:]
    c = pl.program_id(1)
    qbd_f, blk = _block_diag(q_ref[...], N_HEADS)
    qbd = qbd_f.astype(BF16)
    t, jcut = t_ref[...], j_ref[...]
    lane = lax.broadcasted_iota(I32, (1, LANE), 1)

    def selected(key, col):
        return (key > t) | ((key == t) & (col <= jcut))

    @pl.when(c == 0)
    def _():
        sel_new = selected(knew_ref[...], npages * LANE + lane) & (lane == 0)
        on = jnp.max(jnp.where(sel_new, 1.0, 0.0), axis=1, keepdims=True)
        kn = kn_ref[...].astype(BF16).astype(F32)
        s_new = jnp.sum(qbd_f * kn, axis=1, keepdims=True)
        m_ref[...] = jnp.broadcast_to(jnp.where(on > 0.0, s_new, NEG), m_ref.shape)
        l_ref[...] = jnp.broadcast_to(on, l_ref.shape)
        acc_ref[...] = jnp.broadcast_to(on * vn_ref[...].astype(BF16).astype(F32), acc_ref.shape)

    biases = []
    for i in range(pg):
        col = (c * pg + i) * LANE + lane
        bias = jnp.where(selected(keys_ref[i], col), 0.0, NEG)
        biases.append(jnp.broadcast_to(bias, (N_HEADS, LANE)))
    _pages_update(qbd, k_refs, v_refs, biases, m_ref, l_ref, acc_ref)

    @pl.when(c == pl.num_programs(1) - 1)
    def _():
        _page_finish(o_ref, blk, l_ref, acc_ref)


def _dsa_decode(page_table, q3, kn3, vn3, t3, j3, keys4, knew3, cache_kt, cache_vt, pg):
    n, npages = page_table.shape
    page = cache_kt.shape[2]
    assert page == LANE and npages % pg == 0

    def page_map(i):
        return lambda s, c, pt: (pt[s, c * pg + i], 0, 0)

    seq = lambda s, c, pt: (s, 0, 0)
    in_specs = [pl.BlockSpec((None, 1, HW), seq), pl.BlockSpec((None, 1, HW), seq),
                pl.BlockSpec((None, 1, HW), seq), pl.BlockSpec((None, 1, LANE), seq),
                pl.BlockSpec((None, 1, LANE), seq),
                pl.BlockSpec((pg, None, 1, page), lambda s, c, pt: (c, s, 0, 0)),
                pl.BlockSpec((None, 1, LANE), seq)]
    in_specs += [pl.BlockSpec((None, HW, page), page_map(i)) for i in range(pg)]
    in_specs += [pl.BlockSpec((None, HW, page), page_map(i)) for i in range(pg)]
    return pl.pallas_call(
        functools.partial(_dsa_decode_kernel, pg=pg, npages=npages),
        grid_spec=pltpu.PrefetchScalarGridSpec(
            num_scalar_prefetch=1,
            grid=(n, npages // pg),
            in_specs=in_specs,
            out_specs=pl.BlockSpec((None, 1, HW), seq),
            scratch_shapes=[pltpu.VMEM((N_HEADS, LANE), F32), pltpu.VMEM((N_HEADS, LANE), F32),
                            pltpu.VMEM((N_HEADS, HW), F32)]),
        out_shape=jax.ShapeDtypeStruct((n, 1, HW), F32),
        compiler_params=_cparams("arbitrary", "arbitrary"),
        name="dsa_decode",
    )(page_table, q3, kn3, vn3, t3, j3, keys4, knew3, *([cache_kt] * pg), *([cache_vt] * pg))


def kernel(x_prompt, x_sample, cache_fox_k, cache_fox_v, cache_fox_logf, cache_dsa_k, cache_dsa_v, cache_idx_k, cache_mem_k, cache_mem_v, page_table, mem_prompt, w_in, b_fgate, w_fox_up, w_dsa_up, w_mix_out, ln1_g, ln1_b, w_mq, w_mkv, w_mo, ln2_g, ln2_b, w_ffn_gate, w_ffn_up, w_ffn_down, ln3_g, ln3_b):
    assert w_in.shape[0] == 1, "single-layer step"
    b, s, d = x_prompt.shape
    n, t_new, _ = x_sample.shape
    assert t_new == 1
    npages = page_table.shape[1]
    page = cache_fox_k.shape[2]
    past = npages * page
    ksel_p = min(TOPK_MAX, s // 4)
    ksel_s = min(TOPK_MAX, (past + t_new) // 4)
    mem_tokens = mem_prompt.shape[1]
    mw = MEM_HEADS * HEAD_DIM

    fq, fk, fv, fg, dq, dk, dv, iq, ik, iw, ga, gb = jnp.split(w_in[0], IN_OFFSETS, axis=1)
    misc = jnp.concatenate([iw, fg, jnp.zeros((d, LANE - 2 * N_HEADS), F32)], axis=1)
    wcat = jnp.concatenate([fq * QK_SCALE, fk, fv, dq * QK_SCALE, dk, dv, iq, ga, gb, ik, ik, misc],
                           axis=1).astype(BF16)
    bpad = jnp.zeros((1, LANE), F32).at[0, MISC_LOGF:MISC_LOGF + N_HEADS].set(b_fgate[0].astype(F32))
    wf, wd, wm = w_fox_up[0].astype(BF16), w_dsa_up[0].astype(BF16), w_mix_out[0].astype(BF16)
    wq = (w_mq[0] * QK_SCALE).astype(BF16)
    wkv, wo = w_mkv[0].astype(BF16), w_mo[0].astype(BF16)
    wg, wu, wdn = w_ffn_gate[0].astype(BF16), w_ffn_up[0].astype(BF16), w_ffn_down[0].astype(BF16)
    vec = lambda v: v[0].astype(F32).reshape(1, d)
    g1, b1, g2, b2, g3, b3 = map(vec, (ln1_g, ln1_b, ln2_g, ln2_b, ln3_g, ln3_b))

    tm = min(ROW_TILE, s)
    tq = min(Q_TILE, s)
    tk = min(KV_TILE, s)
    xp = x_prompt.reshape(b * s, d)
    (fq_p, fk32, fk16, fv32, fva, fvb, dq_p, dk32, dk16, dv32, dva, dvb, iq_p, ga_p, gb_p, ik32, ik16,
     misc_p) = _in_proj(xp, wcat, bpad, _rope_tables(jnp.arange(s)), tm, s // tm)
    r3 = lambda a: a.reshape(b, s, a.shape[-1])
    crow, ct = _fox_scan(r3(misc_p), tk)
    o_f = _fox_prompt(r3(fq_p), r3(fk16), r3(fva), r3(fvb), crow, ct, tq)
    o_d = _dsa_prompt(r3(iq_p), r3(ik16), r3(misc_p), r3(dq_p), r3(dk16), r3(dva), r3(dvb), tq, tk, ksel_p)
    h1, qm = _merge(o_f.reshape(b * s, HW), o_d.reshape(b * s, HW), ga_p, gb_p, xp, wf, wd, wm, g1, b1, wq, tm)
    mk32, mv32, mk16, mv16 = _mem_kv(mem_prompt.reshape(b * mem_tokens, d), wkv, min(256, b * mem_tokens))
    om = _mem_prompt(qm.reshape(b, s, mw), mk16.reshape(b, mem_tokens, mw), mv16.reshape(b, mem_tokens, mw), tm)
    y_p = _post(h1, om.reshape(b * s, mw), wo, g2, b2, wg, wu, wdn, g3, b3, tm)

    xs = x_sample.reshape(n, d)
    rows = lambda a: jnp.swapaxes(a, 1, 2).reshape(n, 1, HW)
    pos_s = jnp.full((n,), past, jnp.int32)
    (fq_s, fk32s, _, fv32s, _, _, dq_s, dk32s, _, dv32s, _, _, iq_s, ga_s, gb_s, ik32s, _, misc_s) = _in_proj(
        xs, wcat, bpad, _rope_tables(pos_s), n, 1)
    logf_s = misc_s[:, MISC_LOGF:MISC_LOGF + N_HEADS]
    pg = _pages_per_step(npages, DECODE_PAGES_PER_STEP)
    pg_sc = _pages_per_step(npages, SCORE_PAGES_PER_STEP)
    paged = lambda cch: jnp.transpose(cch[0], (0, 2, 3, 1)).reshape(cch.shape[1], HW, page)
    lft = jnp.swapaxes(cache_fox_logf[0], 1, 2)
    ikt = jnp.swapaxes(cache_idx_k[0], 1, 2)
    o_fs = _fox_decode(page_table, fq_s.reshape(n, 1, HW), rows(fk32s), rows(fv32s),
                       logf_s.reshape(n, N_HEADS, 1), paged(cache_fox_k), paged(cache_fox_v), lft, pg)
    keys4, knew3 = _dsa_scores(page_table, iq_s.reshape(n, N_HEADS, HEAD_DIM),
                               misc_s[:, MISC_IW:MISC_IW + N_HEADS].reshape(n, N_HEADS, 1),
                               ik32s[:, :HEAD_DIM].reshape(n, 1, HEAD_DIM), ikt, pg_sc)
    t_s, j_s = _dsa_select(keys4.reshape(npages, n, page), knew3.reshape(n, LANE), ksel_s)
    o_ds = _dsa_decode(page_table, dq_s.reshape(n, 1, HW), rows(dk32s), rows(dv32s),
                       t_s.reshape(n, 1, LANE), j_s.reshape(n, 1, LANE), keys4, knew3,
                       paged(cache_dsa_k), paged(cache_dsa_v), pg)
    h1s, qms = _merge(o_fs.reshape(n, HW), o_ds.reshape(n, HW), ga_s, gb_s, xs, wf, wd, wm, g1, b1, wq, n)
    memt = lambda cch: jnp.transpose(cch[0], (0, 2, 3, 1)).reshape(n, mw, mem_tokens)
    oms = _mem_sample(qms.reshape(n, 1, mw), memt(cache_mem_k), memt(cache_mem_v))
    y_s = _post(h1s, oms.reshape(n, mw), wo, g2, b2, wg, wu, wdn, g3, b3, n)

    def heads(a, lead):
        nb, _, seq = a.shape
        return jnp.transpose(a.reshape(1, nb, N_HEADS, HEAD_DIM, seq), (0, 1, 4, 2, 3)).reshape(
            1, *lead, N_HEADS, HEAD_DIM)

    return (y_p.reshape(b, s, d), y_s.reshape(n, 1, d),
            heads(fk32, (b, s)), heads(fv32, (b, s)),
            misc_p[:, MISC_LOGF:MISC_LOGF + N_HEADS].reshape(1, b, s, N_HEADS),
            heads(dk32, (b, s)), heads(dv32, (b, s)), ik32[:, :HEAD_DIM].reshape(1, b, s, HEAD_DIM),
            mk32.reshape(1, b, mem_tokens, MEM_HEADS, HEAD_DIM), mv32.reshape(1, b, mem_tokens, MEM_HEADS, HEAD_DIM),
            heads(fk32s, (n, 1)), heads(fv32s, (n, 1)), logf_s.reshape(1, n, 1, N_HEADS),
            heads(dk32s, (n, 1)), heads(dv32s, (n, 1)), ik32s[:, :HEAD_DIM].reshape(1, n, 1, HEAD_DIM))
```

```python
import functools

import numpy as np
import jax
import jax.numpy as jnp
from jax import lax
from jax.experimental import pallas as pl
from jax.experimental.pallas import tpu as pltpu

F32 = jnp.float32
BF16 = jnp.bfloat16
I32 = jnp.int32

LANE = 128
VMEM_LIMIT_BYTES = 56 * 1024 * 1024

HEAD_DIM = 64
N_HEADS = 8
MEM_HEADS = 4
ROT_HALF = 8
ROPE_THETA = 500000.0
LN_EPS = 1e-5
TOPK_MAX = 256
DEEPNORM_ALPHA = 2.0 ** 0.25
QK_SCALE = HEAD_DIM ** -0.5
IDX_SCORE_SCALE = (HEAD_DIM ** -0.5) * (N_HEADS ** -0.5)
FF_CHUNK = 256
ROW_TILE = 512
Q_TILE = 512
KV_TILE = 512
DECODE_PAGES_PER_STEP = 32
MEM_SEQS_PER_STEP = 8
SCORE_PAGES_PER_STEP = 64

NEG = -1e30
INT_MIN = -(2 ** 31)

HW = N_HEADS * HEAD_DIM
IN_WIDTHS = [HW, HW, HW, N_HEADS, HW, HW, HW, HW, HEAD_DIM, N_HEADS, 1024, 1024]
IN_OFFSETS = [int(o) for o in np.cumsum(IN_WIDTHS)[:-1]]

C_FQ, C_FK, C_FV, C_DQ, C_DK, C_DV, C_IQ, C_GA, C_GB, C_IK, C_MISC, C_END = (
    0, 512, 1024, 1536, 2048, 2560, 3072, 3584, 4608, 5632, 5760, 5888)
MISC_IW = 0
MISC_LOGF = 8


def _cparams(*sem):
    return pltpu.CompilerParams(dimension_semantics=sem or None, vmem_limit_bytes=VMEM_LIMIT_BYTES)


def _dot(a, b):
    return jnp.dot(a, b, preferred_element_type=F32)


def _dot_nt(a, b):
    return lax.dot_general(a, b, (((1,), (1,)), ((), ())), preferred_element_type=F32)


def _sigmoid(x):
    return 1.0 / (1.0 + jnp.exp(-x))


def _layer_norm(x, g, b):
    mu = jnp.mean(x, axis=-1, keepdims=True)
    xc = x - mu
    var = jnp.mean(xc * xc, axis=-1, keepdims=True)
    return xc * lax.rsqrt(var + LN_EPS) * g + b


def _pages_per_step(npages, want):
    return max(p for p in range(1, min(npages, want) + 1) if npages % p == 0)


def _half_mask(q2, odd):
    lane = lax.broadcasted_iota(I32, q2.shape, 1)
    keep = (lane >= HEAD_DIM) if odd else (lane < HEAD_DIM)
    return jnp.where(keep, q2.astype(F32), 0.0).astype(BF16)


def _inproj_kernel(x_ref, w_ref, cos_ref, sup_ref, sdn_ref, b_ref,
                   fq_ref, fk32_ref, fk16_ref, fv32_ref, fva_ref, fvb_ref,
                   dq_ref, dk32_ref, dk16_ref, dv32_ref, dva_ref, dvb_ref,
                   iq_ref, ga_ref, gb_ref, ik32_ref, ik16_ref, misc_ref):
    xb = x_ref[...].astype(BF16)
    cosv, sup, sdn = cos_ref[...], sup_ref[...], sdn_ref[...]

    def proj(off, width):
        return _dot(xb, w_ref[:, off:off + width])

    def rope(z):
        return (z * cosv + pltpu.roll(z, LANE - ROT_HALF, 1) * sup
                + pltpu.roll(z, ROT_HALF, 1) * sdn)

    def ones_halves(z):
        even = (lax.broadcasted_iota(I32, z.shape, 1) & HEAD_DIM) == 0
        return jnp.where(even, z, 1.0).astype(BF16), jnp.where(even, 1.0, z).astype(BF16)

    fq_ref[...] = proj(C_FQ, HW).astype(BF16)
    z = proj(C_FK, HW)
    fk32_ref[...] = z.T
    fk16_ref[...] = z.astype(BF16)
    z = proj(C_FV, HW)
    fv32_ref[...] = z.T
    fva_ref[...], fvb_ref[...] = ones_halves(z)
    z = proj(C_DQ, HW)
    for g in range(HW // LANE):
        dq_ref[:, g * LANE:(g + 1) * LANE] = rope(z[:, g * LANE:(g + 1) * LANE]).astype(BF16)
    z = proj(C_DK, HW)
    for g in range(HW // LANE):
        r = rope(z[:, g * LANE:(g + 1) * LANE])
        dk32_ref[g * LANE:(g + 1) * LANE, :] = r.T
        dk16_ref[:, g * LANE:(g + 1) * LANE] = r.astype(BF16)
    z = proj(C_DV, HW)
    dv32_ref[...] = z.T
    dva_ref[...], dvb_ref[...] = ones_halves(z)
    z = proj(C_IQ, HW)
    for g in range(HW // LANE):
        iq_ref[:, g * LANE:(g + 1) * LANE] = rope(z[:, g * LANE:(g + 1) * LANE]).astype(BF16)
    ga_ref[...] = proj(C_GA, 1024)
    gb_ref[...] = proj(C_GB, 1024)
    r = rope(proj(C_IK, LANE))
    ik32_ref[...] = r
    ik16_ref[...] = r.astype(BF16)
    z = proj(C_MISC, LANE)
    lane = lax.broadcasted_iota(I32, z.shape, 1)
    t = z + b_ref[...]
    logf = -(jnp.maximum(-t, 0.0) + jnp.log1p(jnp.exp(-jnp.abs(t))))
    misc_ref[...] = jnp.where(lane < MISC_LOGF, z * IDX_SCORE_SCALE,
                              jnp.where(lane < MISC_LOGF + N_HEADS, logf, 0.0))


def _in_proj(x2, wcat, bpad, tables, tm, table_blocks):
    n, d = x2.shape
    seq = table_blocks * tm
    row = lambda i: (i, 0)
    tab = lambda i: (i % table_blocks, 0)
    const = lambda i: (0, 0)
    feature_major = lambda i: (i // table_blocks, 0, i % table_blocks)
    outs = [(HW, BF16), (HW, None), (HW, BF16), (HW, None), (HW, BF16), (HW, BF16),
            (HW, BF16), (HW, None), (HW, BF16), (HW, None), (HW, BF16), (HW, BF16),
            (HW, BF16), (1024, F32), (1024, F32), (LANE, F32), (LANE, BF16), (LANE, F32)]
    return pl.pallas_call(
        _inproj_kernel,
        grid=(n // tm,),
        in_specs=[pl.BlockSpec((tm, d), row),
                  pl.BlockSpec((d, C_END), const, pipeline_mode=pl.Buffered(1)),
                  pl.BlockSpec((tm, LANE), tab), pl.BlockSpec((tm, LANE), tab),
                  pl.BlockSpec((tm, LANE), tab), pl.BlockSpec((1, LANE), const)],
        out_specs=[pl.BlockSpec((None, w, tm), feature_major) if dt is None else pl.BlockSpec((tm, w), row)
                   for w, dt in outs],
        out_shape=[jax.ShapeDtypeStruct((n // seq, w, seq), F32) if dt is None
                   else jax.ShapeDtypeStruct((n, w), dt) for w, dt in outs],
        compiler_params=_cparams("arbitrary"),
        name="in_proj",
    )(x2, wcat, *tables, bpad)


def _rope_tables(pos):
    p = pos.shape[0]
    inv_freq = ROPE_THETA ** (-jnp.arange(ROT_HALF, dtype=F32) / ROT_HALF)
    ang = pos.astype(F32)[:, None] * inv_freq[None, :]
    cos, sin = jnp.cos(ang), jnp.sin(ang)
    zeros = lambda w: jnp.zeros((p, w), F32)
    cos64 = jnp.concatenate([cos, cos, jnp.ones((p, HEAD_DIM - 2 * ROT_HALF), F32)], axis=1)
    sup64 = jnp.concatenate([-sin, zeros(HEAD_DIM - ROT_HALF)], axis=1)
    sdn64 = jnp.concatenate([zeros(ROT_HALF), sin, zeros(HEAD_DIM - 2 * ROT_HALF)], axis=1)
    two = lambda t: jnp.concatenate([t, t], axis=1)
    return two(cos64), two(sup64), two(sdn64)


def _scan_kernel(m_ref, crow_ref, ct_ref):
    x = m_ref[...]
    s = x.shape[0]
    row = lax.broadcasted_iota(I32, x.shape, 0)
    k = 1
    while k < s:
        x = x + jnp.where(row >= k, pltpu.roll(x, k, 0), 0.0)
        k *= 2
    crow_ref[...] = x
    xt = x.T
    nblk, _, tk = ct_ref.shape
    for j in range(nblk):
        ct_ref[j] = xt[MISC_LOGF:MISC_LOGF + N_HEADS, j * tk:(j + 1) * tk]


def _fox_scan(misc3, tk):
    b, s, _ = misc3.shape
    return pl.pallas_call(
        _scan_kernel,
        grid=(b,),
        in_specs=[pl.BlockSpec((None, s, LANE), lambda i: (i, 0, 0))],
        out_specs=[pl.BlockSpec((None, s, LANE), lambda i: (i, 0, 0)),
                   pl.BlockSpec((None, s // tk, N_HEADS, tk), lambda i: (i, 0, 0, 0))],
        out_shape=[jax.ShapeDtypeStruct((b, s, LANE), F32),
                   jax.ShapeDtypeStruct((b, s // tk, N_HEADS, tk), F32)],
        compiler_params=_cparams("arbitrary"),
        name="fox_scan",
    )(misc3)


def _flash_scratch(tq):
    return [pltpu.VMEM((N_HEADS, tq, LANE), BF16),
            pltpu.VMEM((N_HEADS, tq, LANE), F32),
            pltpu.VMEM((N_HEADS // 2, tq, LANE), F32),
            pltpu.VMEM((N_HEADS // 2, tq, LANE), F32)]


def _flash_start(q_ref, qh_ref, m_ref, l_ref, acc_ref):
    for h in range(N_HEADS):
        g = h // 2
        qh_ref[h] = _half_mask(q_ref[:, g * LANE:(g + 1) * LANE], h % 2)
    m_ref[...] = jnp.full(m_ref.shape, NEG, F32)
    l_ref[...] = jnp.zeros(l_ref.shape, F32)
    acc_ref[...] = jnp.zeros(acc_ref.shape, F32)


def _flash_block(qh_ref, m_ref, l_ref, acc_ref, k_ref, va_ref, vb_ref, r0, tk, bias_fn, mask=None):
    tq = qh_ref.shape[1]
    even = lax.broadcasted_iota(I32, (tq, LANE), 1) < HEAD_DIM
    for g in range(N_HEADS // 2):
        gs = slice(g * LANE, (g + 1) * LANE)
        kb = k_ref[pl.ds(r0, tk), gs]
        vs = (va_ref[pl.ds(r0, tk), gs], vb_ref[pl.ds(r0, tk), gs])
        alphas, pvs = [], []
        for e in range(2):
            h = 2 * g + e
            s = _dot_nt(qh_ref[h], kb)
            s = [s[:, c * LANE:(c + 1) * LANE] + bias_fn(h, c) for c in range(tk // LANE)]
            if mask is not None:
                s = [jnp.where(mask[:, c * LANE:(c + 1) * LANE], sc, NEG) for c, sc in enumerate(s)]
            m_old = m_ref[h]
            m_new = jnp.maximum(m_old, jnp.max(functools.reduce(jnp.maximum, s), axis=1, keepdims=True))
            p = jnp.concatenate([jnp.exp(sc - m_new) for sc in s], axis=1)
            m_ref[h] = m_new
            alphas.append(jnp.exp(m_old - m_new))
            pvs.append(_dot(p.astype(BF16), vs[e]))
        acc_ref[g] = jnp.where(even, alphas[0], alphas[1]) * acc_ref[g] + jnp.where(even, pvs[0], pvs[1])
        l_ref[g] = jnp.where(even, alphas[1], alphas[0]) * l_ref[g] + jnp.where(even, pvs[1], pvs[0])


def _flash_finish(o_ref, l_ref, acc_ref):
    for g in range(N_HEADS // 2):
        l = pltpu.roll(l_ref[g], HEAD_DIM, 1)
        o_ref[:, g * LANE:(g + 1) * LANE] = (acc_ref[g] / l).astype(BF16)


def _fox_prompt_kernel(q_ref, k_ref, va_ref, vb_ref, crow_ref, ct_ref, o_ref,
                       qh_ref, m_ref, l_ref, acc_ref, cq_ref):
    qi = pl.program_id(1)
    tq = q_ref.shape[0]
    tk = ct_ref.shape[2]
    _flash_start(q_ref, qh_ref, m_ref, l_ref, acc_ref)
    for h in range(N_HEADS):
        cq_ref[h] = jnp.broadcast_to(crow_ref[:, MISC_LOGF + h:MISC_LOGF + h + 1], (tq, LANE))

    def block(j, mask):
        r0 = pl.multiple_of(j * tk, tk)
        _flash_block(qh_ref, m_ref, l_ref, acc_ref, k_ref, va_ref, vb_ref, r0, tk,
                     lambda h, c: cq_ref[h] - ct_ref[j, h:h + 1, c * LANE:(c + 1) * LANE], mask)

    def body(j, _):
        block(j, None)
        return 0

    nfull = (qi * tq) // tk
    lax.fori_loop(0, nfull, body, 0)
    causal = (nfull * tk + lax.broadcasted_iota(I32, (tq, tk), 1)
              <= qi * tq + lax.broadcasted_iota(I32, (tq, tk), 0))
    block(nfull, causal)
    _flash_finish(o_ref, l_ref, acc_ref)


def _fox_prompt(q, k, va, vb, crow, ct, tq):
    b, s, _ = q.shape
    nblk, tk = ct.shape[1], ct.shape[3]
    assert tk % tq == 0
    qmap = lambda i, j: (i, j, 0)
    full = lambda i, j: (i, 0, 0)
    return pl.pallas_call(
        _fox_prompt_kernel,
        grid=(b, s // tq),
        in_specs=[pl.BlockSpec((None, tq, HW), qmap),
                  pl.BlockSpec((None, s, HW), full), pl.BlockSpec((None, s, HW), full),
                  pl.BlockSpec((None, s, HW), full),
                  pl.BlockSpec((None, tq, LANE), qmap),
                  pl.BlockSpec((None, nblk, N_HEADS, tk), lambda i, j: (i, 0, 0, 0))],
        out_specs=pl.BlockSpec((None, tq, HW), qmap),
        out_shape=jax.ShapeDtypeStruct((b, s, HW), BF16),
        scratch_shapes=_flash_scratch(tq) + [pltpu.VMEM((N_HEADS, tq, LANE), F32)],
        compiler_params=_cparams("arbitrary", "arbitrary"),
        name="fox_prompt",
    )(q, k, va, vb, crow, ct)


COUNT_ROWS = 128
MAX_COUNT_PREDS = 2


def _key_to_float(key):
    return lax.bitcast_convert_type(jnp.where(key < 0, key ^ 0x7FFFFFFF, key), F32)


def _kth_largest(count, ksel, shape):
    ge = lambda cand: (lambda sc, col, rs, f=_key_to_float(cand): sc >= f[rs])
    (c,) = count([lambda sc, col, rs: sc >= 0.0])
    t0 = jnp.where(c >= ksel, jnp.zeros(shape, I32), jnp.full(shape, INT_MIN, I32))

    def body(i, t):
        cand = t | lax.shift_left(jnp.int32(1), 30 - i)
        (c,) = count([ge(cand)])
        return jnp.where(c >= ksel, cand, t)

    t = lax.fori_loop(0, 31, body, t0)
    return jnp.where(t == INT_MIN, -jnp.inf, _key_to_float(t))


def _tie_cut(count, t, ksel, nbits, shape):
    n_gt, n_ge = count([lambda sc, col, rs: sc > t[rs], lambda sc, col, rs: sc >= t[rs]])
    need = ksel - n_gt
    surplus = (n_ge > ksel) & (t > -jnp.inf)
    keep_all = jnp.full(shape, (1 << nbits) - 1, I32)

    def search():
        def body(i, j):
            cand = j | lax.shift_left(jnp.int32(1), nbits - 1 - i)
            (c,) = count([lambda sc, col, rs: (sc == t[rs]) & (col < cand[rs])])
            return jnp.where(c < need, cand, j)
        return lax.fori_loop(0, nbits, body, jnp.zeros(shape, I32))

    return lax.cond(jnp.max(surplus.astype(I32)) > 0, search, lambda: keep_all)


def _dsa_prompt_kernel(iq_ref, ik_ref, misc_ref, q_ref, k_ref, va_ref, vb_ref, o_ref,
                       sc_ref, lhs_ref, qh_ref, m_ref, l_ref, acc_ref, *, ksel, nbits):
    qi = pl.program_id(1)
    tq = q_ref.shape[0]
    kc = sc_ref.shape[2]
    nch = (qi * tq) // kc + 1
    assert kc == tq
    sub = 8

    for h in range(N_HEADS):
        g = h // 2
        lhs_ref[h] = _half_mask(iq_ref[:, g * LANE:(g + 1) * LANE], h % 2)
    wt = misc_ref[...].T
    wrows = [wt[MISC_IW + h:MISC_IW + h + 1, :] for h in range(N_HEADS)]
    key_l = lax.broadcasted_iota(I32, (kc, tq), 0)
    qry_g = qi * tq + lax.broadcasted_iota(I32, (kc, tq), 1)

    def score_chunk(j, _):
        ikc = ik_ref[pl.ds(pl.multiple_of(j * kc, kc), kc), :]
        sc = jnp.zeros((kc, tq), F32)
        for h in range(N_HEADS):
            sc = sc + jnp.maximum(_dot_nt(ikc, lhs_ref[h]), 0.0) * wrows[h]
        sc_ref[j] = jnp.where(j * kc + key_l <= qry_g, sc, -jnp.inf)
        return 0

    lax.fori_loop(0, nch, score_chunk, 0)

    slab_l = lax.broadcasted_iota(I32, (sub, tq), 0)
    every = slice(None)

    def count(preds):
        def body(j, parts):
            parts = list(parts)
            for r in range(kc // sub):
                tile = sc_ref[j, r * sub:(r + 1) * sub, :]
                col = j * kc + r * sub + slab_l
                for i, pred in enumerate(preds):
                    parts[i] = parts[i] + pred(tile, col, every).astype(I32)
            return tuple(parts)

        parts = lax.fori_loop(0, nch, body, tuple(jnp.zeros((sub, tq), I32) for _ in preds))
        return [jnp.sum(part, axis=0, keepdims=True) for part in parts]

    t = _kth_largest(count, ksel, (1, tq))
    jcut = _tie_cut(count, t, ksel, nbits, (1, tq))

    def bias_chunk(j, _):
        col = j * kc + key_l
        tile = sc_ref[j]
        sel = ((tile > t) | ((tile == t) & (col <= jcut))) & (col <= qry_g)
        sc_ref[j] = jnp.where(sel, 0.0, NEG).T
        return 0

    lax.fori_loop(0, nch, bias_chunk, 0)

    _flash_start(q_ref, qh_ref, m_ref, l_ref, acc_ref)

    def block(j, _):
        _flash_block(qh_ref, m_ref, l_ref, acc_ref, k_ref, va_ref, vb_ref, pl.multiple_of(j * kc, kc), kc,
                     lambda h, c: sc_ref[j, :, c * LANE:(c + 1) * LANE])
        return 0

    lax.fori_loop(0, nch, block, 0)
    _flash_finish(o_ref, l_ref, acc_ref)


def _dsa_prompt(iq, ik2, misc3, q, k, va, vb, tq, tk, ksel):
    b, s, _ = q.shape
    assert tk == tq
    nbits = max(1, int(s - 1).bit_length())
    qmap = lambda i, j: (i, j, 0)
    full = lambda i, j: (i, 0, 0)
    return pl.pallas_call(
        functools.partial(_dsa_prompt_kernel, ksel=ksel, nbits=nbits),
        grid=(b, s // tq),
        in_specs=[pl.BlockSpec((None, tq, HW), qmap),
                  pl.BlockSpec((None, s, LANE), full),
                  pl.BlockSpec((None, tq, LANE), qmap),
                  pl.BlockSpec((None, tq, HW), qmap),
                  pl.BlockSpec((None, s, HW), full),
                  pl.BlockSpec((None, s, HW), full),
                  pl.BlockSpec((None, s, HW), full)],
        out_specs=pl.BlockSpec((None, tq, HW), qmap),
        out_shape=jax.ShapeDtypeStruct((b, s, HW), BF16),
        scratch_shapes=[pltpu.VMEM((s // tk, tq, tk), F32),
                        pltpu.VMEM((N_HEADS, tq, LANE), BF16)
                        ] + _flash_scratch(tq),
        compiler_params=_cparams("arbitrary", "arbitrary"),
        name="dsa_prompt",
    )(iq, ik2, misc3, q, k, va, vb)


def _merge_kernel(of_ref, od_ref, ga_ref, gb_ref, x_ref, wf_ref, wd_ref, wm_ref, g_ref, b_ref, wq_ref,
                  h_ref, qm_ref):
    a1 = _dot(of_ref[...].astype(BF16), wf_ref[...])
    a2 = _dot(od_ref[...].astype(BF16), wd_ref[...])
    u = _sigmoid(ga_ref[...]) * a1 + _sigmoid(gb_ref[...]) * a2
    a = _dot(u.astype(BF16), wm_ref[...])
    h = _layer_norm(DEEPNORM_ALPHA * x_ref[...] + a, g_ref[...], b_ref[...])
    h_ref[...] = h
    qm_ref[...] = _dot(h.astype(BF16), wq_ref[...]).astype(BF16)


def _merge(of, od, ga, gb, x2, wf, wd, wm, g1, b1, wq, tm):
    n, d = x2.shape
    mw = wq.shape[1]
    row = lambda i: (i, 0)
    const = lambda i: (0, 0)
    wspec = lambda w: pl.BlockSpec(w.shape, const, pipeline_mode=pl.Buffered(1))
    return pl.pallas_call(
        _merge_kernel,
        grid=(n // tm,),
        in_specs=[pl.BlockSpec((tm, HW), row), pl.BlockSpec((tm, HW), row),
                  pl.BlockSpec((tm, d), row), pl.BlockSpec((tm, d), row), pl.BlockSpec((tm, d), row),
                  wspec(wf), wspec(wd), wspec(wm), wspec(g1), wspec(b1), wspec(wq)],
        out_specs=[pl.BlockSpec((tm, d), row), pl.BlockSpec((tm, mw), row)],
        out_shape=[jax.ShapeDtypeStruct((n, d), F32), jax.ShapeDtypeStruct((n, mw), BF16)],
        compiler_params=_cparams("arbitrary"),
        name="merge_ln1",
    )(of, od, ga, gb, x2, wf, wd, wm, g1, b1, wq)


def _memkv_kernel(x_ref, w_ref, k32_ref, v32_ref, k16_ref, v16_ref):
    z = _dot(x_ref[...].astype(BF16), w_ref[...])
    mw = k32_ref.shape[1]
    k32_ref[...] = z[:, :mw]
    v32_ref[...] = z[:, mw:]
    k16_ref[...] = z[:, :mw].astype(BF16)
    v16_ref[...] = z[:, mw:].astype(BF16)


def _mem_kv(mem2, wkv, tm):
    n, d = mem2.shape
    mw = wkv.shape[1] // 2
    row = lambda i: (i, 0)
    return pl.pallas_call(
        _memkv_kernel,
        grid=(n // tm,),
        in_specs=[pl.BlockSpec((tm, d), row), pl.BlockSpec(wkv.shape, lambda i: (0, 0))],
        out_specs=[pl.BlockSpec((tm, mw), row)] * 4,
        out_shape=[jax.ShapeDtypeStruct((n, mw), F32), jax.ShapeDtypeStruct((n, mw), F32),
                   jax.ShapeDtypeStruct((n, mw), BF16), jax.ShapeDtypeStruct((n, mw), BF16)],
        compiler_params=_cparams("arbitrary"),
        name="mem_kv",
    )(mem2, wkv)


def _mem_prompt_kernel(q_ref, k_ref, v_ref, o_ref):
    tm = q_ref.shape[0]
    lane = lax.broadcasted_iota(I32, (tm, LANE), 1)
    for g in range(q_ref.shape[1] // LANE):
        gs = slice(g * LANE, (g + 1) * LANE)
        q2, kb, vb = q_ref[:, gs], k_ref[:, gs], v_ref[:, gs]
        outs = []
        for e in range(2):
            s = _dot_nt(_half_mask(q2, e), kb)
            p = jnp.exp(s - jnp.max(s, axis=1, keepdims=True))
            outs.append(_dot(p.astype(BF16), vb) / jnp.sum(p, axis=1, keepdims=True))
        o_ref[:, gs] = jnp.where(lane < HEAD_DIM, outs[0], outs[1]).astype(BF16)


def _mem_prompt(qm, mk, mv, tm):
    b, s, mw = qm.shape
    m = mk.shape[1]
    return pl.pallas_call(
        _mem_prompt_kernel,
        grid=(b, s // tm),
        in_specs=[pl.BlockSpec((None, tm, mw), lambda i, j: (i, j, 0)),
                  pl.BlockSpec((None, m, mw), lambda i, j: (i, 0, 0)),
                  pl.BlockSpec((None, m, mw), lambda i, j: (i, 0, 0))],
        out_specs=pl.BlockSpec((None, tm, mw), lambda i, j: (i, j, 0)),
        out_shape=jax.ShapeDtypeStruct((b, s, mw), BF16),
        compiler_params=_cparams("arbitrary", "arbitrary"),
        name="mem_attn_prompt",
    )(qm, mk, mv)


def _block_diag(row_vec, nrows):
    w = row_vec.shape[1]
    rows = lax.broadcasted_iota(I32, (nrows, w), 0)
    lanes = lax.broadcasted_iota(I32, (nrows, w), 1)
    blk = rows == lax.shift_right_logical(lanes, 6)
    return jnp.where(blk, jnp.broadcast_to(row_vec.astype(F32), (nrows, w)), 0.0), blk


def _mem_sample_kernel(q_ref, kt_ref, vt_ref, o_ref):
    for i in range(q_ref.shape[0]):
        qbd, blk = _block_diag(q_ref[i], 8)
        s = _dot(qbd.astype(BF16), kt_ref[i].astype(BF16))
        p = jnp.exp(s - jnp.max(s, axis=1, keepdims=True))
        o = _dot_nt(p.astype(BF16), vt_ref[i].astype(BF16)) / jnp.sum(p, axis=1, keepdims=True)
        o_ref[i] = jnp.sum(jnp.where(blk, o, 0.0), axis=0, keepdims=True)


def _mem_sample(qm3, ck, cv):
    n, mw, m = ck.shape
    g = _pages_per_step(n, MEM_SEQS_PER_STEP)
    return pl.pallas_call(
        _mem_sample_kernel,
        grid=(n // g,),
        in_specs=[pl.BlockSpec((g, 1, mw), lambda i: (i, 0, 0)),
                  pl.BlockSpec((g, mw, m), lambda i: (i, 0, 0)),
                  pl.BlockSpec((g, mw, m), lambda i: (i, 0, 0))],
        out_specs=pl.BlockSpec((g, 1, mw), lambda i: (i, 0, 0)),
        out_shape=jax.ShapeDtypeStruct((n, 1, mw), F32),
        compiler_params=_cparams("arbitrary"),
        name="mem_attn_sample",
    )(qm3, ck, cv)


def _post_kernel(h_ref, om_ref, wo_ref, g2_ref, b2_ref, wg_ref, wu_ref, wdn_ref, g3_ref, b3_ref, y_ref):
    h2 = _layer_norm(DEEPNORM_ALPHA * h_ref[...] + _dot(om_ref[...].astype(BF16), wo_ref[...]),
                     g2_ref[...], b2_ref[...])
    hb = h2.astype(BF16)
    ff = wg_ref.shape[1]
    acc = jnp.zeros(h2.shape, F32)
    for c in range(ff // FF_CHUNK):
        cs = slice(c * FF_CHUNK, (c + 1) * FF_CHUNK)
        gate = _dot(hb, wg_ref[:, cs])
        act = gate * _sigmoid(gate) * _dot(hb, wu_ref[:, cs])
        acc = acc + _dot(act.astype(BF16), wdn_ref[cs, :])
    y_ref[...] = _layer_norm(DEEPNORM_ALPHA * h2 + acc, g3_ref[...], b3_ref[...])


def _post(h1, om, wo, g2, b2, wg, wu, wdn, g3, b3, tm):
    n, d = h1.shape
    row = lambda i: (i, 0)
    const = lambda i: (0, 0)
    wspec = lambda w: pl.BlockSpec(w.shape, const, pipeline_mode=pl.Buffered(1))
    return pl.pallas_call(
        _post_kernel,
        grid=(n // tm,),
        in_specs=[pl.BlockSpec((tm, d), row), pl.BlockSpec((tm, om.shape[1]), row),
                  wspec(wo), wspec(g2), wspec(b2), wspec(wg), wspec(wu), wspec(wdn), wspec(g3), wspec(b3)],
        out_specs=pl.BlockSpec((tm, d), row),
        out_shape=jax.ShapeDtypeStruct((n, d), F32),
        compiler_params=_cparams("arbitrary"),
        name="post_ffn",
    )(h1, om, wo, g2, b2, wg, wu, wdn, g3, b3)


def _pages_update(qbd, kt_refs, vt_refs, biases, m_ref, l_ref, acc_ref):
    s = [_dot(qbd, kt[...].astype(BF16)) + b for kt, b in zip(kt_refs, biases)]
    smax = functools.reduce(jnp.maximum, s)
    m = m_ref[...]
    m_new = jnp.maximum(m, jnp.max(smax, axis=1, keepdims=True))
    alpha = jnp.exp(m - m_new)
    p = [jnp.exp(si - m_new) for si in s]
    l_ref[...] = alpha * l_ref[...] + jnp.sum(functools.reduce(jnp.add, p), axis=1, keepdims=True)
    pv = functools.reduce(jnp.add, [_dot_nt(pi.astype(BF16), vt[...].astype(BF16)) for pi, vt in zip(p, vt_refs)])
    acc_ref[...] = alpha[:, :1] * acc_ref[...] + pv
    m_ref[...] = m_new


def _page_finish(o_ref, blk, l_ref, acc_ref):
    o = acc_ref[...] / l_ref[:, :1]
    o_ref[...] = jnp.sum(jnp.where(blk, o, 0.0), axis=0, keepdims=True)


def _fox_decode_kernel(pt_ref, q_ref, kn_ref, vn_ref, lfn_ref, *rest, pg):
    k_refs, v_refs, lf_refs = rest[:pg], rest[pg:2 * pg], rest[2 * pg:3 * pg]
    o_ref, m_ref, l_ref, acc_ref, suf_ref = rest[3 * pg:]
    c = pl.program_id(1)
    qbd_f, blk = _block_diag(q_ref[...], N_HEADS)
    qbd = qbd_f.astype(BF16)

    @pl.when(c == 0)
    def _():
        kn = kn_ref[...].astype(BF16).astype(F32)
        m_ref[...] = jnp.broadcast_to(jnp.sum(qbd_f * kn, axis=1, keepdims=True), m_ref.shape)
        l_ref[...] = jnp.ones(l_ref.shape, F32)
        acc_ref[...] = jnp.broadcast_to(vn_ref[...].astype(BF16).astype(F32), acc_ref.shape)
        suf_ref[...] = jnp.broadcast_to(lfn_ref[...], suf_ref.shape)

    lane = lax.broadcasted_iota(I32, (N_HEADS, LANE), 1)
    suf = suf_ref[...]
    biases = []
    for i in range(pg):
        y = lf_refs[i][...]
        k = 1
        while k < LANE:
            y = y + jnp.where(lane + k < LANE, pltpu.roll(y, LANE - k, 1), 0.0)
            k *= 2
        biases.append(suf + jnp.where(lane < LANE - 1, pltpu.roll(y, LANE - 1, 1), 0.0))
        suf = suf + y[:, :1]
    suf_ref[...] = suf
    _pages_update(qbd, k_refs, v_refs, biases, m_ref, l_ref, acc_ref)

    @pl.when(c == pl.num_programs(1) - 1)
    def _():
        _page_finish(o_ref, blk, l_ref, acc_ref)


def _fox_decode(page_table, q3, kn3, vn3, lfn3, cache_kt, cache_vt, cache_lft, pg):
    n, npages = page_table.shape
    page = cache_kt.shape[2]
    assert page == LANE and npages % pg == 0

    def page_map(i):
        return lambda s, c, pt: (pt[s, npages - 1 - (c * pg + i)], 0, 0)

    seq = lambda s, c, pt: (s, 0, 0)
    in_specs = [pl.BlockSpec((None, 1, HW), seq), pl.BlockSpec((None, 1, HW), seq),
                pl.BlockSpec((None, 1, HW), seq), pl.BlockSpec((None, N_HEADS, 1), seq)]
    in_specs += [pl.BlockSpec((None, HW, page), page_map(i)) for i in range(pg)]
    in_specs += [pl.BlockSpec((None, HW, page), page_map(i)) for i in range(pg)]
    in_specs += [pl.BlockSpec((None, N_HEADS, page), page_map(i)) for i in range(pg)]
    return pl.pallas_call(
        functools.partial(_fox_decode_kernel, pg=pg),
        grid_spec=pltpu.PrefetchScalarGridSpec(
            num_scalar_prefetch=1,
            grid=(n, npages // pg),
            in_specs=in_specs,
            out_specs=pl.BlockSpec((None, 1, HW), seq),
            scratch_shapes=[pltpu.VMEM((N_HEADS, LANE), F32), pltpu.VMEM((N_HEADS, LANE), F32),
                            pltpu.VMEM((N_HEADS, HW), F32), pltpu.VMEM((N_HEADS, LANE), F32)]),
        out_shape=jax.ShapeDtypeStruct((n, 1, HW), F32),
        compiler_params=_cparams("arbitrary", "arbitrary"),
        name="fox_decode",
    )(page_table, q3, kn3, vn3, lfn3, *([cache_kt] * pg), *([cache_vt] * pg), *([cache_lft] * pg))


def _dsa_score_kernel(pt_ref, iq_ref, w_ref, ikn_ref, *rest, pg):
    pages = rest[:pg]
    keys_ref, knew_ref = rest[pg], rest[pg + 1]
    iq = iq_ref[...]
    w = w_ref[...]
    for i in range(pg):
        r = _dot(iq, pages[i][...].astype(BF16))
        keys_ref[i] = jnp.sum(jnp.maximum(r, 0.0) * w, axis=0, keepdims=True)

    @pl.when(pl.program_id(1) == 0)
    def _():
        ikn = ikn_ref[...].astype(BF16).astype(F32)
        r = jnp.sum(iq.astype(F32) * ikn, axis=1, keepdims=True)
        sc = jnp.sum(jnp.maximum(r, 0.0) * w, axis=0, keepdims=True)
        lane = lax.broadcasted_iota(I32, knew_ref.shape, 1)
        knew_ref[...] = jnp.where(lane == 0, jnp.broadcast_to(sc, knew_ref.shape), -jnp.inf)


def _dsa_scores(page_table, iq3, w3, ikn3, cache_ikt, pg):
    n, npages = page_table.shape
    page = cache_ikt.shape[2]
    assert page == LANE and npages % pg == 0

    def page_map(i):
        return lambda s, c, pt: (pt[s, c * pg + i], 0, 0)

    seq = lambda s, c, pt: (s, 0, 0)
    in_specs = [pl.BlockSpec((None, N_HEADS, HEAD_DIM), seq), pl.BlockSpec((None, N_HEADS, 1), seq),
                pl.BlockSpec((None, 1, HEAD_DIM), seq)]
    in_specs += [pl.BlockSpec((None, HEAD_DIM, page), page_map(i)) for i in range(pg)]
    return pl.pallas_call(
        functools.partial(_dsa_score_kernel, pg=pg),
        grid_spec=pltpu.PrefetchScalarGridSpec(
            num_scalar_prefetch=1,
            grid=(n, npages // pg),
            in_specs=in_specs,
            out_specs=[pl.BlockSpec((pg, None, 1, page), lambda s, c, pt: (c, s, 0, 0)),
                       pl.BlockSpec((None, 1, LANE), seq)]),
        out_shape=[jax.ShapeDtypeStruct((npages, n, 1, page), F32),
                   jax.ShapeDtypeStruct((n, 1, LANE), F32)],
        compiler_params=_cparams("arbitrary", "arbitrary"),
        name="dsa_decode_scores",
    )(page_table, iq3, w3, ikn3, *([cache_ikt] * pg))


def _select_kernel(keys_ref, knew_ref, t_ref, j_ref, cnt_ref, *, ksel, nbits):
    ng, n, _ = keys_ref.shape
    lane = lax.broadcasted_iota(I32, (n, LANE), 1)

    rb = min(COUNT_ROWS, n)
    lane_rb = lax.broadcasted_iota(I32, (rb, LANE), 1)

    def count(preds):
        for r in range(n // rb):
            rs = slice(r * rb, (r + 1) * rb)

            def body(g, parts, rs=rs):
                tile, col = keys_ref[g, rs, :], g * LANE + lane_rb
                return tuple(part + pred(tile, col, rs).astype(I32) for part, pred in zip(parts, preds))

            parts = lax.fori_loop(0, ng, body, tuple(jnp.zeros((rb, LANE), I32) for _ in preds))
            for i, (part, pred) in enumerate(zip(parts, preds)):
                part = part + pred(knew_ref[rs, :], ng * LANE + lane_rb, rs).astype(I32)
                cnt_ref[i, rs, :] = part
        return [jnp.broadcast_to(jnp.sum(cnt_ref[i], axis=1, keepdims=True), cnt_ref.shape[1:])
                for i in range(len(preds))]

    t = _kth_largest(count, ksel, (n, LANE))
    t_ref[...] = t
    j_ref[...] = _tie_cut(count, t, ksel, nbits, (n, LANE))


def _dsa_select(keys3, knew2, ksel):
    ng, n, _ = keys3.shape
    nbits = max(1, int((ng + 1) * LANE - 1).bit_length())
    return pl.pallas_call(
        functools.partial(_select_kernel, ksel=ksel, nbits=nbits),
        out_shape=[jax.ShapeDtypeStruct((n, LANE), F32), jax.ShapeDtypeStruct((n, LANE), I32)],
        scratch_shapes=[pltpu.VMEM((MAX_COUNT_PREDS, n, LANE), I32)],
        compiler_params=_cparams(),
        name="dsa_decode_select",
    )(keys3, knew2)


def _dsa_decode_kernel(pt_ref, q_ref, kn_ref, vn_ref, t_ref, j_ref, keys_ref, knew_ref, *rest, pg, npages):
    k_refs, v_refs = rest[:pg], rest[pg:2 * pg]
    o_ref, m_ref, l_ref, acc_ref = rest[2 * pg:]
    c = pl.program_id(1)
    qbd_f, blk = _block_diag(q_ref[...], N_HEADS)
    qbd = qbd_f.astype(BF16)
    t, jcut = t_ref[...], j_ref[...]
    lane = lax.broadcasted_iota(I32, (1, LANE), 1)

    def selected(key, col):
        return (key > t) | ((key == t) & (col <= jcut))

    @pl.when(c == 0)
    def _():
        sel_new = selected(knew_ref[...], npages * LANE + lane) & (lane == 0)
        on = jnp.max(jnp.where(sel_new, 1.0, 0.0), axis=1, keepdims=True)
        kn = kn_ref[...].astype(BF16).astype(F32)
        s_new = jnp.sum(qbd_f * kn, axis=1, keepdims=True)
        m_ref[...] = jnp.broadcast_to(jnp.where(on > 0.0, s_new, NEG), m_ref.shape)
        l_ref[...] = jnp.broadcast_to(on, l_ref.shape)
        acc_ref[...] = jnp.broadcast_to(on * vn_ref[...].astype(BF16).astype(F32), acc_ref.shape)

    biases = []
    for i in range(pg):
        col = (c * pg + i) * LANE + lane
        bias = jnp.where(selected(keys_ref[i], col), 0.0, NEG)
        biases.append(jnp.broadcast_to(bias, (N_HEADS, LANE)))
    _pages_update(qbd, k_refs, v_refs, biases, m_ref, l_ref, acc_ref)

    @pl.when(c == pl.num_programs(1) - 1)
    def _():
        _page_finish(o_ref, blk, l_ref, acc_ref)


def _dsa_decode(page_table, q3, kn3, vn3, t3, j3, keys4, knew3, cache_kt, cache_vt, pg):
    n, npages = page_table.shape
    page = cache_kt.shape[2]
    assert page == LANE and npages % pg == 0

    def page_map(i):
        return lambda s, c, pt: (pt[s, c * pg + i], 0, 0)

    seq = lambda s, c, pt: (s, 0, 0)
    in_specs = [pl.BlockSpec((None, 1, HW), seq), pl.BlockSpec((None, 1, HW), seq),
                pl.BlockSpec((None, 1, HW), seq), pl.BlockSpec((None, 1, LANE), seq),
                pl.BlockSpec((None, 1, LANE), seq),
                pl.BlockSpec((pg, None, 1, page), lambda s, c, pt: (c, s, 0, 0)),
                pl.BlockSpec((None, 1, LANE), seq)]
    in_specs += [pl.BlockSpec((None, HW, page), page_map(i)) for i in range(pg)]
    in_specs += [pl.BlockSpec((None, HW, page), page_map(i)) for i in range(pg)]
    return pl.pallas_call(
        functools.partial(_dsa_decode_kernel, pg=pg, npages=npages),
        grid_spec=pltpu.PrefetchScalarGridSpec(
            num_scalar_prefetch=1,
            grid=(n, npages // pg),
            in_specs=in_specs,
            out_specs=pl.BlockSpec((None, 1, HW), seq),
            scratch_shapes=[pltpu.VMEM((N_HEADS, LANE), F32), pltpu.VMEM((N_HEADS, LANE), F32),
                            pltpu.VMEM((N_HEADS, HW), F32)]),
        out_shape=jax.ShapeDtypeStruct((n, 1, HW), F32),
        compiler_params=_cparams("arbitrary", "arbitrary"),
        name="dsa_decode",
    )(page_table, q3, kn3, vn3, t3, j3, keys4, knew3, *([cache_kt] * pg), *([cache_vt] * pg))


def kernel(x_prompt, x_sample, cache_fox_k, cache_fox_v, cache_fox_logf, cache_dsa_k, cache_dsa_v, cache_idx_k, cache_mem_k, cache_mem_v, page_table, mem_prompt, w_in, b_fgate, w_fox_up, w_dsa_up, w_mix_out, ln1_g, ln1_b, w_mq, w_mkv, w_mo, ln2_g, ln2_b, w_ffn_gate, w_ffn_up, w_ffn_down, ln3_g, ln3_b):
    assert w_in.shape[0] == 1, "single-layer step"
    b, s, d = x_prompt.shape
    n, t_new, _ = x_sample.shape
    assert t_new == 1
    npages = page_table.shape[1]
    page = cache_fox_k.shape[2]
    past = npages * page
    ksel_p = min(TOPK_MAX, s // 4)
    ksel_s = min(TOPK_MAX, (past + t_new) // 4)
    mem_tokens = mem_prompt.shape[1]
    mw = MEM_HEADS * HEAD_DIM

    fq, fk, fv, fg, dq, dk, dv, iq, ik, iw, ga, gb = jnp.split(w_in[0], IN_OFFSETS, axis=1)
    misc = jnp.concatenate([iw, fg, jnp.zeros((d, LANE - 2 * N_HEADS), F32)], axis=1)
    wcat = jnp.concatenate([fq * QK_SCALE, fk, fv, dq * QK_SCALE, dk, dv, iq, ga, gb, ik, ik, misc],
                           axis=1).astype(BF16)
    bpad = jnp.zeros((1, LANE), F32).at[0, MISC_LOGF:MISC_LOGF + N_HEADS].set(b_fgate[0].astype(F32))
    wf, wd, wm = w_fox_up[0].astype(BF16), w_dsa_up[0].astype(BF16), w_mix_out[0].astype(BF16)
    wq = (w_mq[0] * QK_SCALE).astype(BF16)
    wkv, wo = w_mkv[0].astype(BF16), w_mo[0].astype(BF16)
    wg, wu, wdn = w_ffn_gate[0].astype(BF16), w_ffn_up[0].astype(BF16), w_ffn_down[0].astype(BF16)
    vec = lambda v: v[0].astype(F32).reshape(1, d)
    g1, b1, g2, b2, g3, b3 = map(vec, (ln1_g, ln1_b, ln2_g, ln2_b, ln3_g, ln3_b))

    tm = min(ROW_TILE, s)
    tq = min(Q_TILE, s)
    tk = min(KV_TILE, s)
    xp = x_prompt.reshape(b * s, d)
    (fq_p, fk32, fk16, fv32, fva, fvb, dq_p, dk32, dk16, dv32, dva, dvb, iq_p, ga_p, gb_p, ik32, ik16,
     misc_p) = _in_proj(xp, wcat, bpad, _rope_tables(jnp.arange(s)), tm, s // tm)
    r3 = lambda a: a.reshape(b, s, a.shape[-1])
    crow, ct = _fox_scan(r3(misc_p), tk)
    o_f = _fox_prompt(r3(fq_p), r3(fk16), r3(fva), r3(fvb), crow, ct, tq)
    o_d = _dsa_prompt(r3(iq_p), r3(ik16), r3(misc_p), r3(dq_p), r3(dk16), r3(dva), r3(dvb), tq, tk, ksel_p)
    h1, qm = _merge(o_f.reshape(b * s, HW), o_d.reshape(b * s, HW), ga_p, gb_p, xp, wf, wd, wm, g1, b1, wq, tm)
    mk32, mv32, mk16, mv16 = _mem_kv(mem_prompt.reshape(b * mem_tokens, d), wkv, min(ROW_TILE, b * mem_tokens))
    om = _mem_prompt(qm.reshape(b, s, mw), mk16.reshape(b, mem_tokens, mw), mv16.reshape(b, mem_tokens, mw), tm)
    y_p = _post(h1, om.reshape(b * s, mw), wo, g2, b2, wg, wu, wdn, g3, b3, tm)

    xs = x_sample.reshape(n, d)
    rows = lambda a: jnp.swapaxes(a, 1, 2).reshape(n, 1, HW)
    pos_s = jnp.full((n,), past, jnp.int32)
    (fq_s, fk32s, _, fv32s, _, _, dq_s, dk32s, _, dv32s, _, _, iq_s, ga_s, gb_s, ik32s, _, misc_s) = _in_proj(
        xs, wcat, bpad, _rope_tables(pos_s), n, 1)
    logf_s = misc_s[:, MISC_LOGF:MISC_LOGF + N_HEADS]
    pg = _pages_per_step(npages, DECODE_PAGES_PER_STEP)
    pg_sc = _pages_per_step(npages, SCORE_PAGES_PER_STEP)
    paged = lambda cch: jnp.transpose(cch[0], (0, 2, 3, 1)).reshape(cch.shape[1], HW, page)
    lft = jnp.swapaxes(cache_fox_logf[0], 1, 2)
    ikt = jnp.swapaxes(cache_idx_k[0], 1, 2)
    o_fs = _fox_decode(page_table, fq_s.reshape(n, 1, HW), rows(fk32s), rows(fv32s),
                       logf_s.reshape(n, N_HEADS, 1), paged(cache_fox_k), paged(cache_fox_v), lft, pg)
    keys4, knew3 = _dsa_scores(page_table, iq_s.reshape(n, N_HEADS, HEAD_DIM),
                               misc_s[:, MISC_IW:MISC_IW + N_HEADS].reshape(n, N_HEADS, 1),
                               ik32s[:, :HEAD_DIM].reshape(n, 1, HEAD_DIM), ikt, pg_sc)
    t_s, j_s = _dsa_select(keys4.reshape(npages, n, page), knew3.reshape(n, LANE), ksel_s)
    o_ds = _dsa_decode(page_table, dq_s.reshape(n, 1, HW), rows(dk32s), rows(dv32s),
                       t_s.reshape(n, 1, LANE), j_s.reshape(n, 1, LANE), keys4, knew3,
                       paged(cache_dsa_k), paged(cache_dsa_v), pg)
    h1s, qms = _merge(o_fs.reshape(n, HW), o_ds.reshape(n, HW), ga_s, gb_s, xs, wf, wd, wm, g1, b1, wq, n)
    memt = lambda cch: jnp.transpose(cch[0], (0, 2, 3, 1)).reshape(n, mw, mem_tokens)
    oms = _mem_sample(qms.reshape(n, 1, mw), memt(cache_mem_k), memt(cache_mem_v))
    y_s = _post(h1s, oms.reshape(n, mw), wo, g2, b2, wg, wu, wdn, g3, b3, n)

    def heads(a, lead):
        nb, _, seq = a.shape
        return jnp.transpose(a.reshape(1, nb, N_HEADS, HEAD_DIM, seq), (0, 1, 4, 2, 3)).reshape(
            1, *lead, N_HEADS, HEAD_DIM)

    return (y_p.reshape(b, s, d), y_s.reshape(n, 1, d),
            heads(fk32, (b, s)), heads(fv32, (b, s)),
            misc_p[:, MISC_LOGF:MISC_LOGF + N_HEADS].reshape(1, b, s, N_HEADS),
            heads(dk32, (b, s)), heads(dv32, (b, s)), ik32[:, :HEAD_DIM].reshape(1, b, s, HEAD_DIM),
            mk32.reshape(1, b, mem_tokens, MEM_HEADS, HEAD_DIM), mv32.reshape(1, b, mem_tokens, MEM_HEADS, HEAD_DIM),
            heads(fk32s, (n, 1)), heads(fv32s, (n, 1)), logf_s.reshape(1, n, 1, N_HEADS),
            heads(dk32s, (n, 1)), heads(dv32s, (n, 1)), ik32s[:, :HEAD_DIM].reshape(1, n, 1, HEAD_DIM))
```
